```python
import math
import jax, jax.numpy as jnp
from jax import lax
import numpy as np

D_MODEL = 2048
BATCH = 2
SEQ = 8192
DEPTH = 4
DEC_BATCH = 16
DEC_SEQ = 16
PAST_LEN = 1024

CHUNK = 64
N_MIXERS = 2
N_HEADS = 16
HEAD_DIM = D_MODEL // N_HEADS
SB_SCALE = 1.0 / math.sqrt(HEAD_DIM)
CONV_WIDTH = 31
D_FF = -(-8 * D_MODEL // (3 * 256)) * 256
Q_BLOCK = 128
EPS_RMS = 1e-6
EPS_LN = 1e-5
N_SB = (DEPTH + 1) // 2
N_CONV = DEPTH // 2

kernel_name = 'stickbreak_conformer_stream_step'


def rms_norm(x, g):
    xf = x.astype(jnp.float32)
    y = xf * lax.rsqrt(jnp.mean(xf * xf, axis=-1, keepdims=True) + EPS_RMS)
    return (y * g.astype(jnp.float32)).astype(x.dtype)


def layer_norm(x, g, b):
    xf = x.astype(jnp.float32)
    mu = jnp.mean(xf, axis=-1, keepdims=True)
    var = jnp.mean(jnp.square(xf - mu), axis=-1, keepdims=True)
    y = (xf - mu) * lax.rsqrt(var + EPS_LN)
    return (y * g.astype(jnp.float32) + b.astype(jnp.float32)).astype(x.dtype)


def swiglu_ffn(h, w_gate_up, w_down):
    gate, up = jnp.split(h @ w_gate_up, 2, axis=-1)
    return (jax.nn.silu(gate) * up) @ w_down


def sb_attend(q, k, v, q_pos):
    tk = k.shape[1]
    z = jnp.einsum('bqhd,bkhd->bhqk', q.astype(jnp.float32), k.astype(jnp.float32)) * SB_SCALE
    causal = jnp.arange(tk)[None, :] < q_pos[:, None]
    log_keep = jnp.where(causal, jax.nn.log_sigmoid(-z), 0.0)
    later = lax.cumsum(log_keep, axis=3, reverse=True) - log_keep
    w = jnp.where(causal, jnp.exp(jax.nn.log_sigmoid(z) + later), 0.0)
    return jnp.einsum('bhqk,bkhd->bqhd', w.astype(v.dtype), v)


def sb_qkv(h, w_qkv):
    b, t, _ = h.shape
    qkv = (h @ w_qkv).reshape(b, t, 3, N_HEADS, HEAD_DIM)
    return qkv[:, :, 0], qkv[:, :, 1], qkv[:, :, 2]


def sb_mixer_prompt(h, w_qkv, w_o):
    b, s, _ = h.shape
    q, k, v = sb_qkv(h, w_qkv)
    nb = s // Q_BLOCK
    qb = q.reshape(b, nb, Q_BLOCK, N_HEADS, HEAD_DIM).transpose(1, 0, 2, 3, 4)
    pos = jnp.arange(s).reshape(nb, Q_BLOCK)
    o = lax.map(lambda a: sb_attend(a[0], k, v, a[1]), (qb, pos))
    o = o.transpose(1, 0, 2, 3, 4).reshape(b, s, D_MODEL)
    return o @ w_o, k, v


def sb_mixer_sample(h, k_cache, v_cache, w_qkv, w_o):
    b, t, _ = h.shape
    q, k, v = sb_qkv(h, w_qkv)
    past = k_cache.shape[1]
    k_all = jnp.concatenate([k_cache.astype(k.dtype), k], axis=1)
    v_all = jnp.concatenate([v_cache.astype(v.dtype), v], axis=1)
    o = sb_attend(q, k_all, v_all, past + jnp.arange(t)).reshape(b, t, D_MODEL)
    return o @ w_o, k, v


def conv_module(h, buf, w_pw1, b_pw1, w_dw, b_dw, ln_g, ln_b, w_pw2, b_pw2):
    a, g = jnp.split(h @ w_pw1 + b_pw1, 2, axis=-1)
    u = a * jax.nn.sigmoid(g)
    up = jnp.concatenate([buf.astype(u.dtype), u], axis=1)
    y = lax.conv_general_dilated(up, w_dw[:, None, :].astype(up.dtype), (1,), 'VALID',
                                 dimension_numbers=('NWC', 'WIO', 'NWC'),
                                 feature_group_count=D_MODEL) + b_dw
    y = jax.nn.silu(layer_norm(y, ln_g, ln_b))
    return y @ w_pw2 + b_pw2, up[:, -(CONV_WIDTH - 1):]


def _trunk(x, cache_k, cache_v, state_conv, norm_mix_g, norm_ffn_g, w_qkv, w_o,
           w_pw1, b_pw1, w_dw, b_dw, ln_g, ln_b, w_pw2, b_pw2,
           w_gate_up, w_down, final_norm_g):
    new_k, new_v, new_conv = [], [], []
    for i in range(DEPTH):
        h = rms_norm(x, norm_mix_g[i])
        j = i // N_MIXERS
        if i % N_MIXERS == 0:
            if cache_k is None:
                out, k, v = sb_mixer_prompt(h, w_qkv[j], w_o[j])
            else:
                out, k, v = sb_mixer_sample(h, cache_k[j], cache_v[j], w_qkv[j], w_o[j])
            new_k.append(k)
            new_v.append(v)
        else:
            if state_conv is None:
                buf = jnp.zeros((x.shape[0], CONV_WIDTH - 1, D_MODEL), x.dtype)
            else:
                buf = state_conv[j]
            out, nbuf = conv_module(h, buf, w_pw1[j], b_pw1[j], w_dw[j], b_dw[j],
                                    ln_g[j], ln_b[j], w_pw2[j], b_pw2[j])
            new_conv.append(nbuf)
        x = x + out
        x = x + swiglu_ffn(rms_norm(x, norm_ffn_g[i]), w_gate_up[i], w_down[i])
    return rms_norm(x, final_norm_g), jnp.stack(new_k), jnp.stack(new_v), jnp.stack(new_conv)


def setup_inputs(seed: int = 0) -> dict:
    key = jax.random.key(seed)
    ks = jax.random.split(key, 20)
    D, F, W = D_MODEL, D_FF, CONV_WIDTH

    def nrm(k, shape, scale):
        return jax.random.normal(k, shape, jnp.float32) * scale

    return {
        'x_prompt': nrm(ks[0], (BATCH, SEQ, D), 1.0),
        'x_sample': nrm(ks[1], (DEC_BATCH, DEC_SEQ, D), 1.0),
        'cache_k': nrm(ks[2], (N_SB, DEC_BATCH, PAST_LEN, N_HEADS, HEAD_DIM), 1.0),
        'cache_v': nrm(ks[3], (N_SB, DEC_BATCH, PAST_LEN, N_HEADS, HEAD_DIM), 1.0),
        'state_conv': nrm(ks[4], (N_CONV, DEC_BATCH, W - 1, D), 0.5),
        'norm_mix_g': 1.0 + nrm(ks[5], (DEPTH, D), 0.02),
        'norm_ffn_g': 1.0 + nrm(ks[6], (DEPTH, D), 0.02),
        'w_qkv': nrm(ks[7], (N_SB, D, 3 * D), D ** -0.5),
        'w_o': nrm(ks[8], (N_SB, D, D), D ** -0.5),
        'w_pw1': nrm(ks[9], (N_CONV, D, 2 * D), D ** -0.5),
        'b_pw1': nrm(ks[10], (N_CONV, 2 * D), 0.02),
        'w_dw': nrm(ks[11], (N_CONV, W, D), W ** -0.5),
        'b_dw': nrm(ks[12], (N_CONV, D), 0.02),
        'ln_g': 1.0 + nrm(ks[13], (N_CONV, D), 0.02),
        'ln_b': nrm(ks[14], (N_CONV, D), 0.02),
        'w_pw2': nrm(ks[15], (N_CONV, D, D), D ** -0.5),
        'b_pw2': nrm(ks[16], (N_CONV, D), 0.02),
        'w_gate_up': nrm(ks[17], (DEPTH, D, 2 * F), D ** -0.5),
        'w_down': nrm(ks[18], (DEPTH, F, D), F ** -0.5),
        'final_norm_g': 1.0 + nrm(ks[19], (D,), 0.02),
    }


def reference(x_prompt, x_sample, cache_k, cache_v, state_conv, norm_mix_g, norm_ffn_g,
              w_qkv, w_o, w_pw1, b_pw1, w_dw, b_dw, ln_g, ln_b, w_pw2, b_pw2,
              w_gate_up, w_down, final_norm_g):
    y_prompt, k_p, v_p, conv_p = _trunk(x_prompt, None, None, None, norm_mix_g, norm_ffn_g,
                                        w_qkv, w_o, w_pw1, b_pw1, w_dw, b_dw, ln_g, ln_b,
                                        w_pw2, b_pw2, w_gate_up, w_down, final_norm_g)
    y_sample, k_s, v_s, conv_s = _trunk(x_sample, cache_k, cache_v, state_conv, norm_mix_g,
                                        norm_ffn_g, w_qkv, w_o, w_pw1, b_pw1, w_dw, b_dw,
                                        ln_g, ln_b, w_pw2, b_pw2, w_gate_up, w_down,
                                        final_norm_g)
    return (y_prompt, y_sample, k_p, v_p, conv_p, k_s, v_s, conv_s)
```

```python
import functools
import math

import jax
import jax.numpy as jnp
from jax import lax
from jax.experimental import pallas as pl
from jax.experimental.pallas import tpu as pltpu

D_MODEL = 2048
N_HEADS = 16
HEAD_DIM = D_MODEL // N_HEADS
CONV_WIDTH = 31
CONV_HALO = 32
D_FF = 5632
DEPTH = 4
EPS_RMS = 1e-6
EPS_LN = 1e-5
LOG2E = 1.4426950408889634
Q_SCALE = -LOG2E / math.sqrt(HEAD_DIM)
KEY_BLOCK = 256
LANES = 128

F32 = jnp.float32
BF16 = jnp.bfloat16
VMEM_LIMIT = 56 * 1024 * 1024


def _params(*sem):
    return pltpu.CompilerParams(dimension_semantics=sem, vmem_limit_bytes=VMEM_LIMIT)


def _dot(a, b):
    return jnp.dot(a, b, preferred_element_type=F32)


def _dot_nt(a, b):
    return lax.dot_general(a, b, (((1,), (1,)), ((), ())), preferred_element_type=F32)


def _rms_norm(x, g):
    return x * lax.rsqrt(jnp.mean(x * x, axis=-1, keepdims=True) + EPS_RMS) * g


def _qkv_kernel(x_ref, g_ref, wq_ref, wk_ref, wv_ref, q_ref, kf_ref, vf_ref, kb_ref, vb_ref, h_ref):
    @pl.when(pl.program_id(1) == 0)
    def _():
        h_ref[...] = _rms_norm(x_ref[...], g_ref[...]).astype(BF16)

    h = h_ref[...]
    q_ref[...] = (_dot(h, wq_ref[...]) * Q_SCALE).astype(BF16)
    k = _dot(h, wk_ref[...])
    kf_ref[...] = k
    kb_ref[...] = k.astype(BF16)
    v = _dot(h, wv_ref[...])
    vf_ref[...] = v
    vb_ref[...] = v.astype(BF16)


def _qkv(x, g, w_qkv, *, tm, tn):
    m = x.shape[0]
    nj = D_MODEL // tn
    row = pl.BlockSpec((tm, D_MODEL), lambda i, j: (i, 0))
    out = pl.BlockSpec((tm, tn), lambda i, j: (i, j))
    return pl.pallas_call(
        _qkv_kernel,
        grid=(m // tm, nj),
        in_specs=[
            row,
            pl.BlockSpec((1, D_MODEL), lambda i, j: (0, 0)),
            pl.BlockSpec((D_MODEL, tn), lambda i, j: (0, j)),
            pl.BlockSpec((D_MODEL, tn), lambda i, j: (0, nj + j)),
            pl.BlockSpec((D_MODEL, tn), lambda i, j: (0, 2 * nj + j)),
        ],
        out_specs=[out, out, out, out, out],
        out_shape=[
            jax.ShapeDtypeStruct((m, D_MODEL), BF16),
            jax.ShapeDtypeStruct((m, D_MODEL), F32),
            jax.ShapeDtypeStruct((m, D_MODEL), F32),
            jax.ShapeDtypeStruct((m, D_MODEL), BF16),
            jax.ShapeDtypeStruct((m, D_MODEL), BF16),
        ],
        scratch_shapes=[pltpu.VMEM((tm, D_MODEL), BF16)],
        compiler_params=_params("parallel", "arbitrary"),
        name="qkv",
    )(x, g.reshape(1, D_MODEL), w_qkv, w_qkv, w_qkv)


def _sb_weights(nz2, carry, u, mask):
    bk = nz2.shape[1]
    keep = jnp.minimum(nz2, 0.0) - jnp.log2(1.0 + jnp.exp2(-jnp.abs(nz2)))
    if mask is not None:
        keep = jnp.where(mask, keep, 0.0)
    hi = keep.astype(BF16)
    lo = (keep - hi.astype(F32)).astype(BF16)
    later = _dot(hi, u) + _dot(lo, u)
    total = later[:, 0:1] + keep[:, 0:1]
    later = later + jnp.concatenate([carry] * (bk // LANES), axis=1)
    w = jnp.exp2((keep - nz2) + later)
    if mask is not None:
        w = jnp.where(mask, w, 0.0)
    return w.astype(BF16), jnp.broadcast_to(total, carry.shape)


def _attn_prompt_kernel(q_ref, k_ref, v_ref, u_ref, o_ref, acc_ref, carry_ref, *, tq):
    bk = KEY_BLOCK
    nd = tq // bk
    i = pl.program_id(2)
    acc_ref[...] = jnp.zeros_like(acc_ref)
    carry_ref[...] = jnp.zeros_like(carry_ref)
    u = u_ref[...]

    def step(r0, kb, masked):
        ks = pl.multiple_of(kb * bk, bk)
        nz2 = _dot_nt(q_ref[r0:, :], k_ref[pl.ds(ks, bk), :])
        mask = None
        if masked:
            shape = (tq - r0, bk)
            mask = lax.broadcasted_iota(jnp.int32, shape, 1) < lax.broadcasted_iota(jnp.int32, shape, 0)
        w, total = _sb_weights(nz2, carry_ref[r0:, :], u, mask)
        acc_ref[r0:, :] += _dot(w, v_ref[pl.ds(ks, bk), :])
        carry_ref[r0:, :] += total

    for d in reversed(range(nd)):
        step(d * bk, i * nd + d, True)

    def body(s, c):
        step(0, i * nd - 1 - s, False)
        return c

    lax.fori_loop(0, i * nd, body, 0)
    o_ref[...] = acc_ref[...].astype(BF16)


def _attn_prompt(q, k, v, u, *, batch, seq, tq):
    nq = seq // tq
    return pl.pallas_call(
        functools.partial(_attn_prompt_kernel, tq=tq),
        grid=(batch, N_HEADS, nq),
        in_specs=[
            pl.BlockSpec((tq, HEAD_DIM), lambda b, h, i: (b * nq + i, h)),
            pl.BlockSpec((seq, HEAD_DIM), lambda b, h, i: (b, h)),
            pl.BlockSpec((seq, HEAD_DIM), lambda b, h, i: (b, h)),
            pl.BlockSpec((KEY_BLOCK, KEY_BLOCK), lambda b, h, i: (0, 0)),
        ],
        out_specs=pl.BlockSpec((tq, HEAD_DIM), lambda b, h, i: (b * nq + i, h)),
        out_shape=jax.ShapeDtypeStruct((batch * seq, D_MODEL), BF16),
        scratch_shapes=[pltpu.VMEM((tq, HEAD_DIM), F32), pltpu.VMEM((tq, LANES), F32)],
        compiler_params=_params("parallel", "parallel", "arbitrary"),
        name="attn_prompt",
    )(q, k, v, u)


def _attn_sample_kernel(q_ref, kn_ref, vn_ref, ck_ref, cv_ref, u_ref, o_ref, qblk_ref, acc_ref, carry_ref,
                        *, t_new, n_cache_blocks):
    s = pl.program_id(1)
    rows = N_HEADS * t_new
    u = u_ref[...]

    @pl.when(s == 0)
    def _():
        shape = (rows, D_MODEL)
        row_head = lax.broadcasted_iota(jnp.int32, shape, 0) // t_new
        col_head = lax.broadcasted_iota(jnp.int32, shape, 1) // HEAD_DIM
        qt = jnp.concatenate([q_ref[...].astype(F32)] * N_HEADS, axis=0)
        qblk = jnp.where(row_head == col_head, qt, 0.0).astype(BF16)
        qblk_ref[...] = qblk
        pad = jnp.zeros((LANES - t_new, D_MODEL), BF16)
        kn = jnp.concatenate([kn_ref[...], pad], axis=0)
        vn = jnp.concatenate([vn_ref[...], pad], axis=0)
        nz2 = _dot_nt(qblk, kn)
        shape = (rows, LANES)
        mask = lax.broadcasted_iota(jnp.int32, shape, 1) < lax.broadcasted_iota(jnp.int32, shape, 0) % t_new
        w, total = _sb_weights(nz2, jnp.zeros((rows, LANES), F32), u[:LANES, :LANES], mask)
        acc_ref[...] = _dot(w, vn)
        carry_ref[...] = total

    @pl.when(s > 0)
    def _():
        nz2 = _dot_nt(qblk_ref[...], ck_ref[...].astype(BF16))
        w, total = _sb_weights(nz2, carry_ref[...], u, None)
        acc_ref[...] += _dot(w, cv_ref[...].astype(BF16))
        carry_ref[...] += total

    @pl.when(s == n_cache_blocks)
    def _():
        col_head = lax.broadcasted_iota(jnp.int32, (t_new, D_MODEL), 1) // HEAD_DIM
        out = jnp.zeros((t_new, D_MODEL), F32)
        for h in range(N_HEADS):
            out = jnp.where(col_head == h, acc_ref[h * t_new:(h + 1) * t_new, :], out)
        o_ref[...] = out.astype(BF16)


def _attn_sample(q, kn, vn, cache_k, cache_v, u, *, layer, batch, t_new):
    past = cache_k.shape[2]
    ncb = past // KEY_BLOCK
    rows = N_HEADS * t_new
    ck = cache_k.reshape(cache_k.shape[0], batch, past, D_MODEL)
    cv = cache_v.reshape(cache_v.shape[0], batch, past, D_MODEL)
    new = pl.BlockSpec((t_new, D_MODEL), lambda b, s: (b, 0))
    cache = pl.BlockSpec((None, None, KEY_BLOCK, D_MODEL),
                         lambda b, s: (layer, b, jnp.minimum(ncb - s, ncb - 1), 0))
    return pl.pallas_call(
        functools.partial(_attn_sample_kernel, t_new=t_new, n_cache_blocks=ncb),
        grid=(batch, ncb + 1),
        in_specs=[new, new, new, cache, cache, pl.BlockSpec((KEY_BLOCK, KEY_BLOCK), lambda b, s: (0, 0))],
        out_specs=new,
        out_shape=jax.ShapeDtypeStruct((batch * t_new, D_MODEL), BF16),
        scratch_shapes=[
            pltpu.VMEM((rows, D_MODEL), BF16),
            pltpu.VMEM((rows, D_MODEL), F32),
            pltpu.VMEM((rows, LANES), F32),
        ],
        compiler_params=_params("parallel", "arbitrary"),
        name="attn_sample",
    )(q, kn, vn, ck, cv, u)


def _oproj_kernel(o_ref, w_ref, x_ref, out_ref):
    out_ref[...] = x_ref[...] + _dot(o_ref[...], w_ref[...])


def _oproj(o, w, x, *, tm, tn):
    m = x.shape[0]
    tile = pl.BlockSpec((tm, tn), lambda i, j: (i, j))
    return pl.pallas_call(
        _oproj_kernel,
        grid=(m // tm, D_MODEL // tn),
        in_specs=[
            pl.BlockSpec((tm, D_MODEL), lambda i, j: (i, 0)),
            pl.BlockSpec((D_MODEL, tn), lambda i, j: (0, j)),
            tile,
        ],
        out_specs=tile,
        out_shape=jax.ShapeDtypeStruct((m, D_MODEL), F32),
        compiler_params=_params("parallel", "parallel"),
        name="oproj",
    )(o, w, x)


def _ffn_kernel(x_ref, g_ref, wg_ref, wu_ref, wd_ref, fg_ref, out_ref, h_ref, *, final_norm):
    f = pl.program_id(1)

    @pl.when(f == 0)
    def _():
        x = x_ref[...]
        h_ref[...] = _rms_norm(x, g_ref[...]).astype(BF16)
        out_ref[...] = x

    h = h_ref[...]
    gate = _dot(h, wg_ref[...])
    up = _dot(h, wu_ref[...])
    act = (gate * jax.nn.sigmoid(gate) * up).astype(BF16)
    out_ref[...] += _dot(act, wd_ref[...])

    if final_norm:
        @pl.when(f == pl.num_programs(1) - 1)
        def _():
            out_ref[...] = _rms_norm(out_ref[...], fg_ref[...])


def _ffn(x, g, w_gate_up, w_down, final_g, *, tm, tf, final_norm):
    m = x.shape[0]
    nf = D_FF // tf
    row = pl.BlockSpec((tm, D_MODEL), lambda i, f: (i, 0))
    vec = pl.BlockSpec((1, D_MODEL), lambda i, f: (0, 0))
    return pl.pallas_call(
        functools.partial(_ffn_kernel, final_norm=final_norm),
        grid=(m // tm, nf),
        in_specs=[
            row,
            vec,
            pl.BlockSpec((D_MODEL, tf), lambda i, f: (0, f)),
            pl.BlockSpec((D_MODEL, tf), lambda i, f: (0, nf + f)),
            pl.BlockSpec((tf, D_MODEL), lambda i, f: (f, 0)),
            vec,
        ],
        out_specs=row,
        out_shape=jax.ShapeDtypeStruct((m, D_MODEL), F32),
        scratch_shapes=[pltpu.VMEM((tm, D_MODEL), BF16)],
        compiler_params=_params("parallel", "arbitrary"),
        name="ffn",
    )(x, g.reshape(1, D_MODEL), w_gate_up, w_gate_up, w_down, final_g.reshape(1, D_MODEL))


def _pw1_kernel(x_ref, g_ref, wa_ref, wg_ref, ba_ref, bg_ref, u_ref, h_ref):
    @pl.when(pl.program_id(1) == 0)
    def _():
        h_ref[...] = _rms_norm(x_ref[...], g_ref[...]).astype(BF16)

    h = h_ref[...]
    a = _dot(h, wa_ref[...]) + ba_ref[...]
    gate = _dot(h, wg_ref[...]) + bg_ref[...]
    u_ref[...] = a * jax.nn.sigmoid(gate)


def _pw1(x, g, w, b, *, tm, tn):
    m = x.shape[0]
    nj = D_MODEL // tn
    b = b.reshape(1, 2 * D_MODEL)
    return pl.pallas_call(
        _pw1_kernel,
        grid=(m // tm, nj),
        in_specs=[
            pl.BlockSpec((tm, D_MODEL), lambda i, j: (i, 0)),
            pl.BlockSpec((1, D_MODEL), lambda i, j: (0, 0)),
            pl.BlockSpec((D_MODEL, tn), lambda i, j: (0, j)),
            pl.BlockSpec((D_MODEL, tn), lambda i, j: (0, nj + j)),
            pl.BlockSpec((1, tn), lambda i, j: (0, j)),
            pl.BlockSpec((1, tn), lambda i, j: (0, nj + j)),
        ],
        out_specs=pl.BlockSpec((tm, tn), lambda i, j: (i, j)),
        out_shape=jax.ShapeDtypeStruct((m, D_MODEL), F32),
        scratch_shapes=[pltpu.VMEM((tm, D_MODEL), BF16)],
        compiler_params=_params("parallel", "arbitrary"),
        name="pw1_glu",
    )(x, g.reshape(1, D_MODEL), w, w, b, b)


def _conv_kernel(u_ref, halo_ref, x_ref, wdw_ref, bdw_ref, lng_ref, lnb_ref, w2_ref, b2_ref, out_ref,
                 ext_ref, y_ref, *, tm, halo_is_state):
    halo = halo_ref[...]
    if not halo_is_state:
        halo = jnp.where(pl.program_id(1) > 0, halo, 0.0)
    ext_ref[0:CONV_HALO, :] = halo
    ext_ref[CONV_HALO:, :] = u_ref[...]
    shift = CONV_HALO - (CONV_WIDTH - 1)

    def lane_block(c, carry):
        cols = pl.ds(pl.multiple_of(c * LANES, LANES), LANES)
        acc = jnp.zeros((tm, LANES), F32)
        for w in range(CONV_WIDTH):
            acc = acc + ext_ref[w + shift:w + shift + tm, cols] * wdw_ref[w:w + 1, cols]
        y_ref[:, cols] = acc + bdw_ref[:, cols]
        return carry

    lax.fori_loop(0, D_MODEL // LANES, lane_block, 0)

    y = y_ref[...]
    mu = jnp.mean(y, axis=-1, keepdims=True)
    yc = y - mu
    var = jnp.mean(yc * yc, axis=-1, keepdims=True)
    yn = yc * lax.rsqrt(var + EPS_LN) * lng_ref[...] + lnb_ref[...]
    act = (yn * jax.nn.sigmoid(yn)).astype(BF16)
    out_ref[...] = x_ref[...] + _dot(act, w2_ref[...]) + b2_ref[...]


def _conv(u, state, x, w_dw, b_dw, ln_g, ln_b, w_pw2, b_pw2, *, tm):
    batch, seq, _ = u.shape
    halo_is_state = state is not None
    if halo_is_state:
        halo_src = jnp.pad(state, ((0, 0), (CONV_HALO - (CONV_WIDTH - 1), 0), (0, 0)))
        halo_spec = pl.BlockSpec((None, CONV_HALO, D_MODEL), lambda b, i: (b, 0, 0))
    else:
        halo_src = u
        per_tile = tm // CONV_HALO
        halo_spec = pl.BlockSpec((None, CONV_HALO, D_MODEL),
                                 lambda b, i: (b, jnp.maximum(i * per_tile - 1, 0), 0))
    tile = pl.BlockSpec((None, tm, D_MODEL), lambda b, i: (b, i, 0))
    vec = pl.BlockSpec((1, D_MODEL), lambda b, i: (0, 0))
    w_dw = jnp.pad(w_dw, ((0, CONV_HALO - CONV_WIDTH), (0, 0)))
    return pl.pallas_call(
        functools.partial(_conv_kernel, tm=tm, halo_is_state=halo_is_state),
        grid=(batch, seq // tm),
        in_specs=[
            tile, halo_spec, tile,
            pl.BlockSpec((CONV_HALO, D_MODEL), lambda b, i: (0, 0)),
            vec, vec, vec,
            pl.BlockSpec((D_MODEL, D_MODEL), lambda b, i: (0, 0)),
            vec,
        ],
        out_specs=tile,
        out_shape=jax.ShapeDtypeStruct((batch, seq, D_MODEL), F32),
        scratch_shapes=[pltpu.VMEM((CONV_HALO + tm, D_MODEL), F32), pltpu.VMEM((tm, D_MODEL), F32)],
        compiler_params=_params("parallel", "arbitrary"),
        name="conv_pw2",
    )(u, halo_src, x, w_dw, b_dw.reshape(1, -1), ln_g.reshape(1, -1), ln_b.reshape(1, -1), w_pw2,
      b_pw2.reshape(1, -1))


def _trunk(x, cache_k, cache_v, state_conv, p, *, tm, tq, conv_tm):
    batch, seq, _ = x.shape
    m = batch * seq
    x2 = x.reshape(m, D_MODEL)
    new_k, new_v, new_conv = [], [], []
    for layer in range(DEPTH):
        j = layer // 2
        if layer % 2 == 0:
            q, kf, vf, kb, vb = _qkv(x2, p["norm_mix_g"][layer], p["w_qkv"][j], tm=tm, tn=512)
            if cache_k is None:
                o = _attn_prompt(q, kb, vb, p["tri"], batch=batch, seq=seq, tq=tq)
            else:
                o = _attn_sample(q, kb, vb, cache_k, cache_v, p["tri"], layer=j, batch=batch, t_new=seq)
            x2 = _oproj(o, p["w_o"][j], x2, tm=tm, tn=512)
            new_k.append(kf.reshape(batch, seq, N_HEADS, HEAD_DIM))
            new_v.append(vf.reshape(batch, seq, N_HEADS, HEAD_DIM))
        else:
            u = _pw1(x2, p["norm_mix_g"][layer], p["w_pw1"][j], p["b_pw1"][j], tm=tm, tn=512)
            u = u.reshape(batch, seq, D_MODEL)
            if state_conv is None:
                state = None
                new_conv.append(u[:, seq - (CONV_WIDTH - 1):])
            else:
                state = state_conv[j]
                new_conv.append(jnp.concatenate([state, u], axis=1)[:, -(CONV_WIDTH - 1):])
            x2 = _conv(u, state, x2.reshape(batch, seq, D_MODEL), p["w_dw"][j], p["b_dw"][j],
                       p["ln_g"][j], p["ln_b"][j], p["w_pw2"][j], p["b_pw2"][j], tm=conv_tm).reshape(m, D_MODEL)
        x2 = _ffn(x2, p["norm_ffn_g"][layer], p["w_gate_up"][layer], p["w_down"][layer], p["final_norm_g"],
                  tm=tm, tf=512, final_norm=(layer == DEPTH - 1))
    return x2.reshape(batch, seq, D_MODEL), jnp.stack(new_k), jnp.stack(new_v), jnp.stack(new_conv)


def kernel(x_prompt, x_sample, cache_k, cache_v, state_conv, norm_mix_g, norm_ffn_g, w_qkv, w_o, w_pw1, b_pw1,
           w_dw, b_dw, ln_g, ln_b, w_pw2, b_pw2, w_gate_up, w_down, final_norm_g):
    idx = lax.broadcasted_iota(jnp.int32, (KEY_BLOCK, KEY_BLOCK), 0)
    p = dict(
        norm_mix_g=norm_mix_g, norm_ffn_g=norm_ffn_g, final_norm_g=final_norm_g,
        w_qkv=w_qkv.astype(BF16), w_o=w_o.astype(BF16), w_pw1=w_pw1.astype(BF16), b_pw1=b_pw1,
        w_dw=w_dw, b_dw=b_dw, ln_g=ln_g, ln_b=ln_b, w_pw2=w_pw2.astype(BF16), b_pw2=b_pw2,
        w_gate_up=w_gate_up.astype(BF16), w_down=w_down.astype(BF16),
        tri=(idx > idx.T).astype(BF16),
    )
    y_p, k_p, v_p, conv_p = _trunk(x_prompt, None, None, None, p, tm=512, tq=1024, conv_tm=256)
    y_s, k_s, v_s, conv_s = _trunk(x_sample, cache_k, cache_v, state_conv, p, tm=256, tq=None, conv_tm=16)
    return (y_p, y_s, k_p, v_p, conv_p, k_s, v_s, conv_s)
```

```python
import functools
import math

import jax
import jax.numpy as jnp
from jax import lax
from jax.experimental import pallas as pl
from jax.experimental.pallas import tpu as pltpu

D_MODEL = 2048
N_HEADS = 16
HEAD_DIM = D_MODEL // N_HEADS
CONV_WIDTH = 31
CONV_HALO = 32
SUBLANES = 8
LANES = 128
D_FF = 5632
DEPTH = 4
N_SB = 2
EPS_RMS = 1e-6
EPS_LN = 1e-5
LOG2E = 1.4426950408889634
Q_SCALE = -LOG2E / math.sqrt(HEAD_DIM)
KEY_BLOCK = 256
SB_EXIT_LOG2 = -150.0
SB_DONE = -1e30
SB_BLOCKS_PER_STEP = 2
HEADS_PER_QKV_STEP = 8

F32 = jnp.float32
BF16 = jnp.bfloat16
VMEM_LIMIT = 56 * 1024 * 1024


def _params(*sem):
    return pltpu.CompilerParams(dimension_semantics=sem, vmem_limit_bytes=VMEM_LIMIT)


def _dot(a, b):
    return jnp.dot(a, b, preferred_element_type=F32)


def _dot_nt(a, b):
    return lax.dot_general(a, b, (((1,), (1,)), ((), ())), preferred_element_type=F32)


def _rms_norm(x, g):
    return x * lax.rsqrt(jnp.mean(x * x, axis=-1, keepdims=True) + EPS_RMS) * g


def _layer_vec(layer, width=D_MODEL):
    return pl.BlockSpec((None, 1, width), lambda *_: (layer, 0, 0))


def _qkv_kernel(x_ref, g_ref, wq_ref, wk_ref, wv_ref, kin_ref, vin_ref,
                q_ref, kf_ref, vf_ref, kb_ref, vb_ref, h_ref):
    del kin_ref, vin_ref

    @pl.when(pl.program_id(1) == 0)
    def _():
        h_ref[...] = _rms_norm(x_ref[...], g_ref[...]).astype(BF16)

    h = h_ref[...]
    q_ref[...] = (_dot(h, wq_ref[...]) * Q_SCALE).astype(BF16)
    k = _dot(h, wk_ref[...])
    kf_ref[...] = pltpu.einshape("m(hd)->mhd", k, h=HEADS_PER_QKV_STEP)
    kb_ref[...] = k.astype(BF16)
    v = _dot(h, wv_ref[...])
    vf_ref[...] = pltpu.einshape("m(hd)->mhd", v, h=HEADS_PER_QKV_STEP)
    vb_ref[...] = v.astype(BF16)


def _qkv(x, g, w_qkv, k_all, v_all, *, layer, tm):
    m = x.shape[0]
    tn = HEADS_PER_QKV_STEP * HEAD_DIM
    nj = D_MODEL // tn
    out = pl.BlockSpec((tm, tn), lambda i, j: (i, j))
    slab = pl.BlockSpec((None, tm, HEADS_PER_QKV_STEP, HEAD_DIM), lambda i, j: (layer, i, j, 0))
    untouched = pl.BlockSpec(memory_space=pl.ANY)
    return pl.pallas_call(
        _qkv_kernel,
        grid=(m // tm, nj),
        in_specs=[
            pl.BlockSpec((tm, D_MODEL), lambda i, j: (i, 0)),
            _layer_vec(2 * layer),
            pl.BlockSpec((None, D_MODEL, tn), lambda i, j: (layer, 0, j)),
            pl.BlockSpec((None, D_MODEL, tn), lambda i, j: (layer, 0, nj + j)),
            pl.BlockSpec((None, D_MODEL, tn), lambda i, j: (layer, 0, 2 * nj + j)),
            untouched, untouched,
        ],
        out_specs=[out, slab, slab, out, out],
        out_shape=[
            jax.ShapeDtypeStruct((m, D_MODEL), BF16),
            jax.ShapeDtypeStruct(k_all.shape, F32),
            jax.ShapeDtypeStruct(v_all.shape, F32),
            jax.ShapeDtypeStruct((m, D_MODEL), BF16),
            jax.ShapeDtypeStruct((m, D_MODEL), BF16),
        ],
        input_output_aliases={5: 1, 6: 2},
        scratch_shapes=[pltpu.VMEM((tm, D_MODEL), BF16)],
        compiler_params=_params("parallel", "arbitrary"),
        name="qkv",
    )(x, g, w_qkv, w_qkv, w_qkv, k_all, v_all)


def _sb_keep(nz2, mask):
    keep = jnp.minimum(nz2, 0.0) - jnp.log2(1.0 + jnp.exp2(-jnp.abs(nz2)))
    if mask is not None:
        keep = jnp.where(mask, keep, 0.0)
    hi = keep.astype(BF16)
    lo = (keep - hi.astype(F32)).astype(BF16)
    return keep, jnp.concatenate([hi, lo], axis=1)


def _sb_later(hilo, u):
    return _dot(hilo, jnp.concatenate([u, u], axis=0))


def _sb_finish(nz2, keep, later, carry, mask):
    bk = nz2.shape[1]
    total = later[:, 0:1] + keep[:, 0:1]
    later = later + jnp.concatenate([carry] * (bk // LANES), axis=1)
    w = jnp.exp2((keep - nz2) + later)
    if mask is not None:
        w = jnp.where(mask, w, 0.0)
    return w.astype(BF16), jnp.broadcast_to(total, carry.shape)


def _sb_weights(nz2, carry, u, mask):
    keep, hilo = _sb_keep(nz2, mask)
    return _sb_finish(nz2, keep, _sb_later(hilo, u), carry, mask)


def _attn_prompt_kernel(q_ref, k_ref, v_ref, u_ref, o_ref, acc_ref, carry_ref, *, tq):
    bk = KEY_BLOCK
    n_sub = tq // bk
    g = pl.program_id(2)
    acc_ref[...] = jnp.zeros_like(acc_ref)
    carry_ref[...] = jnp.zeros_like(carry_ref)
    u = u_ref[...]
    diag_mask = lax.broadcasted_iota(jnp.int32, (bk, bk), 1) < lax.broadcasted_iota(jnp.int32, (bk, bk), 0)

    def step(s, first):
        chains = [(r, j) for r in range(n_sub) for j in range(SB_BLOCKS_PER_STEP)]
        rows = {c: slice(c[0] * bk, (c[0] + 1) * bk) for c in chains}
        block = {c: g * n_sub + c[0] - (s * SB_BLOCKS_PER_STEP + c[1]) for c in chains}
        keys = {c: pl.ds(pl.multiple_of(jnp.maximum(block[c], 0) * bk, bk), bk) for c in chains}
        mask = {c: diag_mask if (first and c[1] == 0) else None for c in chains}
        nz2 = {c: _dot_nt(q_ref[rows[c], :], k_ref[keys[c], :]) for c in chains}
        keep = {c: _sb_keep(nz2[c], mask[c]) for c in chains}
        later = {c: _sb_later(keep[c][1], u) for c in chains}
        for r in range(n_sub):
            carry = carry_ref[rows[(r, 0)], :]
            ws = []
            for j in range(SB_BLOCKS_PER_STEP):
                c = (r, j)
                w, total = _sb_finish(nz2[c], keep[c][0], later[c], carry, mask[c])
                ws.append(w)
                carry = jnp.where(block[c] <= 0, SB_DONE, carry + total)
            carry_ref[rows[(r, 0)], :] = carry
            vs = [v_ref[keys[(r, j)], :] for j in range(SB_BLOCKS_PER_STEP)]
            acc_ref[rows[(r, 0)], :] += _dot(jnp.concatenate(ws, axis=1), jnp.concatenate(vs, axis=0))
        return jnp.max(carry_ref[...])

    def cond(state):
        return state[1] > SB_EXIT_LOG2

    def body(state):
        s = state[0]
        return s + 1, step(s, False)

    lax.while_loop(cond, body, (jnp.int32(1), step(0, True)))
    o_ref[...] = acc_ref[...].astype(BF16)


def _attn_prompt(q, k, v, u, *, batch, seq, tq):
    nq = seq // tq
    return pl.pallas_call(
        functools.partial(_attn_prompt_kernel, tq=tq),
        grid=(batch, N_HEADS, nq),
        in_specs=[
            pl.BlockSpec((tq, HEAD_DIM), lambda b, h, i: (b * nq + i, h)),
            pl.BlockSpec((seq, HEAD_DIM), lambda b, h, i: (b, h)),
            pl.BlockSpec((seq, HEAD_DIM), lambda b, h, i: (b, h)),
            pl.BlockSpec((KEY_BLOCK, KEY_BLOCK), lambda b, h, i: (0, 0)),
        ],
        out_specs=pl.BlockSpec((tq, HEAD_DIM), lambda b, h, i: (b * nq + i, h)),
        out_shape=jax.ShapeDtypeStruct((batch * seq, D_MODEL), BF16),
        scratch_shapes=[pltpu.VMEM((tq, HEAD_DIM), F32), pltpu.VMEM((tq, LANES), F32)],
        compiler_params=_params("parallel", "parallel", "arbitrary"),
        name="attn_prompt",
    )(q, k, v, u)


def _attn_sample_kernel(q_ref, kn_ref, vn_ref, ck_ref, cv_ref, u_ref, o_ref, qblk_ref, acc_ref, carry_ref,
                        *, t_new, n_cache_blocks):
    s = pl.program_id(1)
    rows = N_HEADS * t_new
    u = u_ref[...]

    @pl.when(s == 0)
    def _():
        shape = (rows, D_MODEL)
        row_head = lax.broadcasted_iota(jnp.int32, shape, 0) // t_new
        col_head = lax.broadcasted_iota(jnp.int32, shape, 1) // HEAD_DIM
        qt = jnp.concatenate([q_ref[...].astype(F32)] * N_HEADS, axis=0)
        qblk = jnp.where(row_head == col_head, qt, 0.0).astype(BF16)
        qblk_ref[...] = qblk
        pad = jnp.zeros((LANES - t_new, D_MODEL), BF16)
        kn = jnp.concatenate([kn_ref[...], pad], axis=0)
        vn = jnp.concatenate([vn_ref[...], pad], axis=0)
        nz2 = _dot_nt(qblk, kn)
        shape = (rows, LANES)
        mask = lax.broadcasted_iota(jnp.int32, shape, 1) < lax.broadcasted_iota(jnp.int32, shape, 0) % t_new
        w, total = _sb_weights(nz2, jnp.zeros((rows, LANES), F32), u[:LANES, :LANES], mask)
        acc_ref[...] = _dot(w, vn)
        carry_ref[...] = total

    @pl.when(s > 0)
    def _():
        nz2 = _dot_nt(qblk_ref[...], ck_ref[...].astype(BF16))
        w, total = _sb_weights(nz2, carry_ref[...], u, None)
        acc_ref[...] += _dot(w, cv_ref[...].astype(BF16))
        carry_ref[...] += total

    @pl.when(s == n_cache_blocks)
    def _():
        col_head = lax.broadcasted_iota(jnp.int32, (t_new, D_MODEL), 1) // HEAD_DIM
        out = jnp.zeros((t_new, D_MODEL), F32)
        for h in range(N_HEADS):
            out = jnp.where(col_head == h, acc_ref[h * t_new:(h + 1) * t_new, :], out)
        o_ref[...] = out.astype(BF16)


def _attn_sample(q, kn, vn, cache_k, cache_v, u, *, layer, batch, t_new):
    past = cache_k.shape[2]
    ncb = past // KEY_BLOCK
    rows = N_HEADS * t_new
    ck = cache_k.reshape(cache_k.shape[0], batch, past, D_MODEL)
    cv = cache_v.reshape(cache_v.shape[0], batch, past, D_MODEL)
    new = pl.BlockSpec((t_new, D_MODEL), lambda b, s: (b, 0))
    cache = pl.BlockSpec((None, None, KEY_BLOCK, D_MODEL),
                         lambda b, s: (layer, b, jnp.minimum(ncb - s, ncb - 1), 0))
    return pl.pallas_call(
        functools.partial(_attn_sample_kernel, t_new=t_new, n_cache_blocks=ncb),
        grid=(batch, ncb + 1),
        in_specs=[new, new, new, cache, cache, pl.BlockSpec((KEY_BLOCK, KEY_BLOCK), lambda b, s: (0, 0))],
        out_specs=new,
        out_shape=jax.ShapeDtypeStruct((batch * t_new, D_MODEL), BF16),
        scratch_shapes=[
            pltpu.VMEM((rows, D_MODEL), BF16),
            pltpu.VMEM((rows, D_MODEL), F32),
            pltpu.VMEM((rows, LANES), F32),
        ],
        compiler_params=_params("parallel", "arbitrary"),
        name="attn_sample",
    )(q, kn, vn, ck, cv, u)


def _ffn_kernel(*refs, final_norm, with_oproj):
    if with_oproj:
        x_ref, o_ref, wo_ref, g_ref, wg_ref, wu_ref, wd_ref, fg_ref, out_ref, h_ref = refs
    else:
        x_ref, g_ref, wg_ref, wu_ref, wd_ref, fg_ref, out_ref, h_ref = refs
    f = pl.program_id(1)

    @pl.when(f == 0)
    def _():
        x = x_ref[...]
        if with_oproj:
            x = x + _dot(o_ref[...], wo_ref[...])
        h_ref[...] = _rms_norm(x, g_ref[...]).astype(BF16)
        out_ref[...] = x

    h = h_ref[...]
    gate = _dot(h, wg_ref[...])
    up = _dot(h, wu_ref[...])
    act = (gate * jax.nn.sigmoid(gate) * up).astype(BF16)
    out_ref[...] += _dot(act, wd_ref[...])

    if final_norm:
        @pl.when(f == pl.num_programs(1) - 1)
        def _():
            out_ref[...] = _rms_norm(out_ref[...], fg_ref[...])


def _ffn(x, o, p, *, layer, tm, tf):
    m = x.shape[0]
    nf = D_FF // tf
    with_oproj = o is not None
    row = pl.BlockSpec((tm, D_MODEL), lambda i, f: (i, 0))
    specs, args = [row], [x]
    if with_oproj:
        specs += [row, pl.BlockSpec((None, D_MODEL, D_MODEL), lambda i, f: (layer // 2, 0, 0),
                                    pipeline_mode=pl.Buffered(1))]
        args += [o, p["w_o"]]
    specs += [
        _layer_vec(layer),
        pl.BlockSpec((None, D_MODEL, tf), lambda i, f: (layer, 0, f)),
        pl.BlockSpec((None, D_MODEL, tf), lambda i, f: (layer, 0, nf + f)),
        pl.BlockSpec((None, tf, D_MODEL), lambda i, f: (layer, f, 0)),
        _layer_vec(0),
    ]
    args += [p["norm_ffn_g"], p["w_gate_up"], p["w_gate_up"], p["w_down"], p["final_norm_g"]]
    return pl.pallas_call(
        functools.partial(_ffn_kernel, final_norm=(layer == DEPTH - 1), with_oproj=with_oproj),
        grid=(m // tm, nf),
        in_specs=specs,
        out_specs=row,
        out_shape=jax.ShapeDtypeStruct((m, D_MODEL), F32),
        scratch_shapes=[pltpu.VMEM((tm, D_MODEL), BF16)],
        compiler_params=_params("parallel", "arbitrary"),
        name="ffn",
    )(*args)


def _pw1_kernel(x_ref, g_ref, wa_ref, wg_ref, ba_ref, bg_ref, u_ref, h_ref):
    @pl.when(pl.program_id(1) == 0)
    def _():
        h_ref[...] = _rms_norm(x_ref[...], g_ref[...]).astype(BF16)

    h = h_ref[...]
    a = _dot(h, wa_ref[...]) + ba_ref[...]
    gate = _dot(h, wg_ref[...]) + bg_ref[...]
    u_ref[...] = a * jax.nn.sigmoid(gate)


def _pw1(x, p, *, layer, tm, tn):
    m = x.shape[0]
    nj = D_MODEL // tn
    j_layer = layer // 2
    return pl.pallas_call(
        _pw1_kernel,
        grid=(m // tm, nj),
        in_specs=[
            pl.BlockSpec((tm, D_MODEL), lambda i, j: (i, 0)),
            _layer_vec(layer),
            pl.BlockSpec((None, D_MODEL, tn), lambda i, j: (j_layer, 0, j)),
            pl.BlockSpec((None, D_MODEL, tn), lambda i, j: (j_layer, 0, nj + j)),
            pl.BlockSpec((None, 1, tn), lambda i, j: (j_layer, 0, j)),
            pl.BlockSpec((None, 1, tn), lambda i, j: (j_layer, 0, nj + j)),
        ],
        out_specs=pl.BlockSpec((tm, tn), lambda i, j: (i, j)),
        out_shape=jax.ShapeDtypeStruct((m, D_MODEL), F32),
        scratch_shapes=[pltpu.VMEM((tm, D_MODEL), BF16)],
        compiler_params=_params("parallel", "arbitrary"),
        name="pw1_glu",
    )(x, p["norm_mix_g"], p["w_pw1"], p["w_pw1"], p["b_pw1"], p["b_pw1"])


def _conv_kernel(u_ref, halo_ref, x_ref, wdw_ref, bdw_ref, lng_ref, lnb_ref, w2_ref, b2_ref, out_ref,
                 ext_ref, y_ref, *, tm, halo_is_state):
    halo = halo_ref[...]
    if not halo_is_state:
        halo = jnp.where(pl.program_id(1) > 0, halo, 0.0)
    ext_ref[0:CONV_HALO, :] = halo
    ext_ref[CONV_HALO:, :] = u_ref[...]
    n_ext = CONV_HALO + tm
    first = CONV_HALO - (CONV_WIDTH - 1)

    def lane_block(c, carry):
        cols = pl.ds(pl.multiple_of(c * LANES, LANES), LANES)
        x = ext_ref[:, cols]
        acc = jnp.zeros((tm, LANES), F32)
        for phase in range(SUBLANES):
            xp = x if phase == 0 else pltpu.roll(x, n_ext - phase, axis=0)
            for base in range(0, CONV_HALO + SUBLANES, SUBLANES):
                w = base + phase - first
                if 0 <= w < CONV_WIDTH:
                    acc = acc + xp[base:base + tm] * wdw_ref[w:w + 1, cols]
        y_ref[:, cols] = acc + bdw_ref[:, cols]
        return carry

    lax.fori_loop(0, D_MODEL // LANES, lane_block, 0)

    y = y_ref[...]
    mu = jnp.mean(y, axis=-1, keepdims=True)
    yc = y - mu
    var = jnp.mean(yc * yc, axis=-1, keepdims=True)
    yn = yc * lax.rsqrt(var + EPS_LN) * lng_ref[...] + lnb_ref[...]
    act = (yn * jax.nn.sigmoid(yn)).astype(BF16)
    out_ref[...] = x_ref[...] + _dot(act, w2_ref[...]) + b2_ref[...]


def _conv(u, state, x, p, *, layer, tm):
    batch, seq, _ = u.shape
    j_layer = layer // 2
    halo_is_state = state is not None
    if halo_is_state:
        halo_src = state
        halo_spec = pl.BlockSpec((None, CONV_HALO, D_MODEL), lambda b, i: (b, 0, 0))
    else:
        halo_src = u
        per_tile = tm // CONV_HALO
        halo_spec = pl.BlockSpec((None, CONV_HALO, D_MODEL),
                                 lambda b, i: (b, jnp.maximum(i * per_tile - 1, 0), 0))
    tile = pl.BlockSpec((None, tm, D_MODEL), lambda b, i: (b, i, 0))
    vec = _layer_vec(j_layer)
    return pl.pallas_call(
        functools.partial(_conv_kernel, tm=tm, halo_is_state=halo_is_state),
        grid=(batch, seq // tm),
        in_specs=[
            tile, halo_spec, tile,
            pl.BlockSpec((None, CONV_HALO, D_MODEL), lambda b, i: (j_layer, 0, 0)),
            vec, vec, vec,
            pl.BlockSpec((None, D_MODEL, D_MODEL), lambda b, i: (j_layer, 0, 0), pipeline_mode=pl.Buffered(1)),
            vec,
        ],
        out_specs=tile,
        out_shape=jax.ShapeDtypeStruct((batch, seq, D_MODEL), F32),
        scratch_shapes=[pltpu.VMEM((CONV_HALO + tm, D_MODEL), F32), pltpu.VMEM((tm, D_MODEL), F32)],
        compiler_params=_params("parallel", "arbitrary"),
        name="conv_pw2",
    )(u, halo_src, x, p["w_dw"], p["b_dw"], p["ln_g"], p["ln_b"], p["w_pw2"], p["b_pw2"])


def _trunk(x, cache_k, cache_v, state_conv, p, *, tm, tq, conv_tm):
    batch, seq, _ = x.shape
    m = batch * seq
    x2 = x.reshape(m, D_MODEL)
    k_all = jnp.zeros((N_SB, m, N_HEADS, HEAD_DIM), F32)
    v_all = jnp.zeros((N_SB, m, N_HEADS, HEAD_DIM), F32)
    new_conv = []
    for layer in range(DEPTH):
        j = layer // 2
        if layer % 2 == 0:
            q, k_all, v_all, kb, vb = _qkv(x2, p["norm_mix_g"], p["w_qkv"], k_all, v_all, layer=j, tm=tm)
            if cache_k is None:
                o = _attn_prompt(q, kb, vb, p["tri"], batch=batch, seq=seq, tq=tq)
            else:
                o = _attn_sample(q, kb, vb, cache_k, cache_v, p["tri"], layer=j, batch=batch, t_new=seq)
        else:
            o = None
            u = _pw1(x2, p, layer=layer, tm=tm, tn=512).reshape(batch, seq, D_MODEL)
            if state_conv is None:
                state = None
                new_conv.append(u[:, seq - (CONV_WIDTH - 1):])
            else:
                hist = jnp.concatenate([state_conv[j], u], axis=1)
                new_conv.append(hist[:, -(CONV_WIDTH - 1):])
                state = jnp.pad(state_conv[j], ((0, 0), (CONV_HALO - (CONV_WIDTH - 1), 0), (0, 0)))
            x2 = _conv(u, state, x2.reshape(batch, seq, D_MODEL), p, layer=layer, tm=conv_tm).reshape(m, D_MODEL)
        x2 = _ffn(x2, o, p, layer=layer, tm=tm, tf=512)
    shape = (N_SB, batch, seq, N_HEADS, HEAD_DIM)
    return x2.reshape(batch, seq, D_MODEL), k_all.reshape(shape), v_all.reshape(shape), jnp.stack(new_conv)


def kernel(x_prompt, x_sample, cache_k, cache_v, state_conv, norm_mix_g, norm_ffn_g, w_qkv, w_o, w_pw1, b_pw1,
           w_dw, b_dw, ln_g, ln_b, w_pw2, b_pw2, w_gate_up, w_down, final_norm_g):
    idx = lax.broadcasted_iota(jnp.int32, (KEY_BLOCK, KEY_BLOCK), 0)

    def rows(a):
        return a.reshape(a.shape[0], 1, a.shape[1])

    p = dict(
        norm_mix_g=rows(norm_mix_g), norm_ffn_g=rows(norm_ffn_g), final_norm_g=final_norm_g.reshape(1, 1, D_MODEL),
        w_qkv=w_qkv.astype(BF16), w_o=w_o.astype(BF16), w_pw1=w_pw1.astype(BF16), b_pw1=rows(b_pw1),
        w_dw=jnp.pad(w_dw, ((0, 0), (0, CONV_HALO - CONV_WIDTH), (0, 0))), b_dw=rows(b_dw),
        ln_g=rows(ln_g), ln_b=rows(ln_b), w_pw2=w_pw2.astype(BF16), b_pw2=rows(b_pw2),
        w_gate_up=w_gate_up.astype(BF16), w_down=w_down.astype(BF16),
        tri=(idx > idx.T).astype(BF16),
    )
    y_p, k_p, v_p, conv_p = _trunk(x_prompt, None, None, None, p, tm=512, tq=1024, conv_tm=256)
    y_s, k_s, v_s, conv_s = _trunk(x_sample, cache_k, cache_v, state_conv, p, tm=256, tq=None, conv_tm=16)
    return (y_p, y_s, k_p, v_p, conv_p, k_s, v_s, conv_s)
```

```python
import functools
import math

import jax
import jax.numpy as jnp
from jax import lax
from jax.experimental import pallas as pl
from jax.experimental.pallas import tpu as pltpu

D_MODEL = 2048
N_HEADS = 16
HEAD_DIM = D_MODEL // N_HEADS
CONV_WIDTH = 31
CONV_HALO = 32
SUBLANES = 8
LANES = 128
D_FF = 5632
DEPTH = 4
N_SB = 2
EPS_RMS = 1e-6
EPS_LN = 1e-5
LOG2E = 1.4426950408889634
Q_SCALE = -LOG2E / math.sqrt(HEAD_DIM)
KEY_BLOCK = 256
SB_EXIT_LOG2 = -150.0
SB_DONE = -1e30
SB_BLOCKS_PER_STEP = 2
HEADS_PER_QKV_STEP = 8

F32 = jnp.float32
BF16 = jnp.bfloat16
VMEM_LIMIT = 56 * 1024 * 1024


def _params(*sem):
    return pltpu.CompilerParams(dimension_semantics=sem, vmem_limit_bytes=VMEM_LIMIT)


def _dot(a, b):
    return jnp.dot(a, b, preferred_element_type=F32)


def _dot_nt(a, b):
    return lax.dot_general(a, b, (((1,), (1,)), ((), ())), preferred_element_type=F32)


def _rms_norm(x, g):
    return x * lax.rsqrt(jnp.mean(x * x, axis=-1, keepdims=True) + EPS_RMS) * g


def _layer_vec(layer, width=D_MODEL):
    return pl.BlockSpec((None, 1, width), lambda *_: (layer, 0, 0))


def _qkv_kernel(x_ref, g_ref, wq_ref, wk_ref, wv_ref, kin_ref, vin_ref,
                q_ref, kf_ref, vf_ref, kb_ref, vb_ref, h_ref):
    del kin_ref, vin_ref

    @pl.when(pl.program_id(1) == 0)
    def _():
        h_ref[...] = _rms_norm(x_ref[...], g_ref[...]).astype(BF16)

    h = h_ref[...]
    q_ref[...] = (_dot(h, wq_ref[...]) * Q_SCALE).astype(BF16)
    k = _dot(h, wk_ref[...])
    kf_ref[...] = pltpu.einshape("m(hd)->mhd", k, h=HEADS_PER_QKV_STEP)
    kb_ref[...] = k.astype(BF16)
    v = _dot(h, wv_ref[...])
    vf_ref[...] = pltpu.einshape("m(hd)->mhd", v, h=HEADS_PER_QKV_STEP)
    vb_ref[...] = v.astype(BF16)


def _qkv(x, g, w_qkv, k_all, v_all, *, layer, tm):
    m = x.shape[0]
    tn = HEADS_PER_QKV_STEP * HEAD_DIM
    nj = D_MODEL // tn
    out = pl.BlockSpec((tm, tn), lambda i, j: (i, j))
    slab = pl.BlockSpec((None, tm, HEADS_PER_QKV_STEP, HEAD_DIM), lambda i, j: (layer, i, j, 0))
    untouched = pl.BlockSpec(memory_space=pl.ANY)
    return pl.pallas_call(
        _qkv_kernel,
        grid=(m // tm, nj),
        in_specs=[
            pl.BlockSpec((tm, D_MODEL), lambda i, j: (i, 0)),
            _layer_vec(2 * layer),
            pl.BlockSpec((None, D_MODEL, tn), lambda i, j: (layer, 0, j)),
            pl.BlockSpec((None, D_MODEL, tn), lambda i, j: (layer, 0, nj + j)),
            pl.BlockSpec((None, D_MODEL, tn), lambda i, j: (layer, 0, 2 * nj + j)),
            untouched, untouched,
        ],
        out_specs=[out, slab, slab, out, out],
        out_shape=[
            jax.ShapeDtypeStruct((m, D_MODEL), BF16),
            jax.ShapeDtypeStruct(k_all.shape, F32),
            jax.ShapeDtypeStruct(v_all.shape, F32),
            jax.ShapeDtypeStruct((m, D_MODEL), BF16),
            jax.ShapeDtypeStruct((m, D_MODEL), BF16),
        ],
        input_output_aliases={5: 1, 6: 2},
        scratch_shapes=[pltpu.VMEM((tm, D_MODEL), BF16)],
        compiler_params=_params("parallel", "arbitrary"),
        name="qkv",
    )(x, g, w_qkv, w_qkv, w_qkv, k_all, v_all)


def _sb_keep(nz2, mask):
    keep = jnp.minimum(nz2, 0.0) - jnp.log2(1.0 + jnp.exp2(-jnp.abs(nz2)))
    if mask is not None:
        keep = jnp.where(mask, keep, 0.0)
    hi = keep.astype(BF16)
    lo = (keep - hi.astype(F32)).astype(BF16)
    return keep, jnp.concatenate([hi, lo], axis=1)


def _sb_later(hilo, u):
    return _dot(hilo, jnp.concatenate([u, u], axis=0))


def _sb_finish(nz2, keep, later, carry, mask):
    bk = nz2.shape[1]
    total = later[:, 0:1] + keep[:, 0:1]
    later = later + jnp.concatenate([carry] * (bk // LANES), axis=1)
    w = jnp.exp2((keep - nz2) + later)
    if mask is not None:
        w = jnp.where(mask, w, 0.0)
    return w.astype(BF16), jnp.broadcast_to(total, carry.shape)


def _sb_weights(nz2, carry, u, mask):
    keep, hilo = _sb_keep(nz2, mask)
    return _sb_finish(nz2, keep, _sb_later(hilo, u), carry, mask)


def _attn_prompt_kernel(q_ref, k_ref, v_ref, u_ref, o_ref, acc_ref, carry_ref, *, tq):
    bk = KEY_BLOCK
    n_sub = tq // bk
    g = pl.program_id(2)
    acc_ref[...] = jnp.zeros_like(acc_ref)
    carry_ref[...] = jnp.zeros_like(carry_ref)
    u = u_ref[...]
    diag_mask = lax.broadcasted_iota(jnp.int32, (bk, bk), 1) < lax.broadcasted_iota(jnp.int32, (bk, bk), 0)

    def step(s, first):
        chains = [(r, j) for r in range(n_sub) for j in range(SB_BLOCKS_PER_STEP)]
        rows = {c: slice(c[0] * bk, (c[0] + 1) * bk) for c in chains}
        block = {c: g * n_sub + c[0] - (s * SB_BLOCKS_PER_STEP + c[1]) for c in chains}
        keys = {c: pl.ds(pl.multiple_of(jnp.maximum(block[c], 0) * bk, bk), bk) for c in chains}
        mask = {c: diag_mask if (first and c[1] == 0) else None for c in chains}
        nz2 = {c: _dot_nt(q_ref[rows[c], :], k_ref[keys[c], :]) for c in chains}
        keep = {c: _sb_keep(nz2[c], mask[c]) for c in chains}
        later = {c: _sb_later(keep[c][1], u) for c in chains}
        for r in range(n_sub):
            carry = carry_ref[rows[(r, 0)], :]
            ws = []
            for j in range(SB_BLOCKS_PER_STEP):
                c = (r, j)
                w, total = _sb_finish(nz2[c], keep[c][0], later[c], carry, mask[c])
                ws.append(w)
                carry = jnp.where(block[c] <= 0, SB_DONE, carry + total)
            carry_ref[rows[(r, 0)], :] = carry
            vs = [v_ref[keys[(r, j)], :] for j in range(SB_BLOCKS_PER_STEP)]
            acc_ref[rows[(r, 0)], :] += _dot(jnp.concatenate(ws, axis=1), jnp.concatenate(vs, axis=0))
        return jnp.max(carry_ref[...])

    def cond(state):
        return state[1] > SB_EXIT_LOG2

    def body(state):
        s = state[0]
        return s + 1, step(s, False)

    lax.while_loop(cond, body, (jnp.int32(1), step(0, True)))
    o_ref[...] = acc_ref[...].astype(BF16)


def _attn_prompt(q, k, v, u, *, batch, seq, tq):
    nq = seq // tq
    return pl.pallas_call(
        functools.partial(_attn_prompt_kernel, tq=tq),
        grid=(batch, N_HEADS, nq),
        in_specs=[
            pl.BlockSpec((tq, HEAD_DIM), lambda b, h, i: (b * nq + i, h)),
            pl.BlockSpec((seq, HEAD_DIM), lambda b, h, i: (b, h)),
            pl.BlockSpec((seq, HEAD_DIM), lambda b, h, i: (b, h)),
            pl.BlockSpec((KEY_BLOCK, KEY_BLOCK), lambda b, h, i: (0, 0)),
        ],
        out_specs=pl.BlockSpec((tq, HEAD_DIM), lambda b, h, i: (b * nq + i, h)),
        out_shape=jax.ShapeDtypeStruct((batch * seq, D_MODEL), BF16),
        scratch_shapes=[pltpu.VMEM((tq, HEAD_DIM), F32), pltpu.VMEM((tq, LANES), F32)],
        compiler_params=_params("parallel", "parallel", "arbitrary"),
        name="attn_prompt",
    )(q, k, v, u)


def _attn_sample_kernel(q_ref, kn_ref, vn_ref, ck_ref, cv_ref, u_ref, o_ref, *, t_new):
    bk = KEY_BLOCK
    rows = N_HEADS * t_new
    n_blocks = ck_ref.shape[0] // bk
    u = u_ref[...]

    shape = (rows, D_MODEL)
    row_head = lax.broadcasted_iota(jnp.int32, shape, 0) // t_new
    col_head = lax.broadcasted_iota(jnp.int32, shape, 1) // HEAD_DIM
    qt = jnp.concatenate([q_ref[...].astype(F32)] * N_HEADS, axis=0)
    qblk = jnp.where(row_head == col_head, qt, 0.0).astype(BF16)
    pad = jnp.zeros((LANES - t_new, D_MODEL), BF16)
    kn = jnp.concatenate([kn_ref[...], pad], axis=0)
    vn = jnp.concatenate([vn_ref[...], pad], axis=0)
    kc = pltpu.einshape("shd->s(hd)", ck_ref[...]).astype(BF16)
    vc = pltpu.einshape("shd->s(hd)", cv_ref[...]).astype(BF16)

    shape = (rows, LANES)
    mask_new = lax.broadcasted_iota(jnp.int32, shape, 1) < lax.broadcasted_iota(jnp.int32, shape, 0) % t_new
    nz2_new = _dot_nt(qblk, kn)
    nz2_all = _dot_nt(qblk, kc)
    nz2 = [nz2_all[:, c * bk:(c + 1) * bk] for c in range(n_blocks)]
    keep_new = _sb_keep(nz2_new, mask_new)
    keep = [_sb_keep(z, None) for z in nz2]
    later_new = _sb_later(keep_new[1], u[:LANES, :LANES])
    later = [_sb_later(k[1], u) for k in keep]

    w_new, carry = _sb_finish(nz2_new, keep_new[0], later_new, jnp.zeros((rows, LANES), F32), mask_new)
    ws = [None] * n_blocks
    for c in reversed(range(n_blocks)):
        ws[c], total = _sb_finish(nz2[c], keep[c][0], later[c], carry, None)
        carry = carry + total
    acc = _dot(jnp.concatenate(ws + [w_new], axis=1), jnp.concatenate([vc, vn], axis=0))

    col_head = lax.broadcasted_iota(jnp.int32, (t_new, D_MODEL), 1) // HEAD_DIM
    out = jnp.zeros((t_new, D_MODEL), F32)
    for h in range(N_HEADS):
        out = jnp.where(col_head == h, acc[h * t_new:(h + 1) * t_new, :], out)
    o_ref[...] = out.astype(BF16)


def _attn_sample(q, kn, vn, cache_k, cache_v, u, *, layer, batch, t_new):
    past = cache_k.shape[2]
    new = pl.BlockSpec((t_new, D_MODEL), lambda b: (b, 0))
    cache = pl.BlockSpec((None, None, past, N_HEADS, HEAD_DIM), lambda b: (layer, b, 0, 0, 0))
    return pl.pallas_call(
        functools.partial(_attn_sample_kernel, t_new=t_new),
        grid=(batch,),
        in_specs=[new, new, new, cache, cache, pl.BlockSpec((KEY_BLOCK, KEY_BLOCK), lambda b: (0, 0))],
        out_specs=new,
        out_shape=jax.ShapeDtypeStruct((batch * t_new, D_MODEL), BF16),
        compiler_params=_params("parallel"),
        name="attn_sample",
    )(q, kn, vn, cache_k, cache_v, u)


def _ffn_kernel(*refs, final_norm, with_oproj):
    if with_oproj:
        x_ref, o_ref, wo_ref, g_ref, wg_ref, wu_ref, wd_ref, fg_ref, out_ref, h_ref = refs
    else:
        x_ref, g_ref, wg_ref, wu_ref, wd_ref, fg_ref, out_ref, h_ref = refs
    f = pl.program_id(1)

    @pl.when(f == 0)
    def _():
        x = x_ref[...]
        if with_oproj:
            x = x + _dot(o_ref[...], wo_ref[...])
        h_ref[...] = _rms_norm(x, g_ref[...]).astype(BF16)
        out_ref[...] = x

    h = h_ref[...]
    gate = _dot(h, wg_ref[...])
    up = _dot(h, wu_ref[...])
    act = (gate * jax.nn.sigmoid(gate) * up).astype(BF16)
    out_ref[...] += _dot(act, wd_ref[...])

    if final_norm:
        @pl.when(f == pl.num_programs(1) - 1)
        def _():
            out_ref[...] = _rms_norm(out_ref[...], fg_ref[...])


def _ffn(x, o, p, *, layer, tm, tf):
    m = x.shape[0]
    nf = D_FF // tf
    with_oproj = o is not None
    row = pl.BlockSpec((tm, D_MODEL), lambda i, f: (i, 0))
    specs, args = [row], [x]
    if with_oproj:
        specs += [row, pl.BlockSpec((None, D_MODEL, D_MODEL), lambda i, f: (layer // 2, 0, 0),
                                    pipeline_mode=pl.Buffered(1))]
        args += [o, p["w_o"]]
    specs += [
        _layer_vec(layer),
        pl.BlockSpec((None, D_MODEL, tf), lambda i, f: (layer, 0, f)),
        pl.BlockSpec((None, D_MODEL, tf), lambda i, f: (layer, 0, nf + f)),
        pl.BlockSpec((None, tf, D_MODEL), lambda i, f: (layer, f, 0)),
        _layer_vec(0),
    ]
    args += [p["norm_ffn_g"], p["w_gate_up"], p["w_gate_up"], p["w_down"], p["final_norm_g"]]
    return pl.pallas_call(
        functools.partial(_ffn_kernel, final_norm=(layer == DEPTH - 1), with_oproj=with_oproj),
        grid=(m // tm, nf),
        in_specs=specs,
        out_specs=row,
        out_shape=jax.ShapeDtypeStruct((m, D_MODEL), F32),
        scratch_shapes=[pltpu.VMEM((tm, D_MODEL), BF16)],
        compiler_params=_params("parallel", "arbitrary"),
        name="ffn",
    )(*args)


def _pw1_kernel(x_ref, g_ref, wa_ref, wg_ref, ba_ref, bg_ref, u_ref, h_ref):
    @pl.when(pl.program_id(1) == 0)
    def _():
        h_ref[...] = _rms_norm(x_ref[...], g_ref[...]).astype(BF16)

    h = h_ref[...]
    a = _dot(h, wa_ref[...]) + ba_ref[...]
    gate = _dot(h, wg_ref[...]) + bg_ref[...]
    u_ref[...] = a * jax.nn.sigmoid(gate)


def _pw1(x, p, *, layer, tm, tn):
    m = x.shape[0]
    nj = D_MODEL // tn
    j_layer = layer // 2
    return pl.pallas_call(
        _pw1_kernel,
        grid=(m // tm, nj),
        in_specs=[
            pl.BlockSpec((tm, D_MODEL), lambda i, j: (i, 0)),
            _layer_vec(layer),
            pl.BlockSpec((None, D_MODEL, tn), lambda i, j: (j_layer, 0, j)),
            pl.BlockSpec((None, D_MODEL, tn), lambda i, j: (j_layer, 0, nj + j)),
            pl.BlockSpec((None, 1, tn), lambda i, j: (j_layer, 0, j)),
            pl.BlockSpec((None, 1, tn), lambda i, j: (j_layer, 0, nj + j)),
        ],
        out_specs=pl.BlockSpec((tm, tn), lambda i, j: (i, j)),
        out_shape=jax.ShapeDtypeStruct((m, D_MODEL), F32),
        scratch_shapes=[pltpu.VMEM((tm, D_MODEL), BF16)],
        compiler_params=_params("parallel", "arbitrary"),
        name="pw1_glu",
    )(x, p["norm_mix_g"], p["w_pw1"], p["w_pw1"], p["b_pw1"], p["b_pw1"])


def _conv_kernel(u_ref, halo_ref, x_ref, wdw_ref, bdw_ref, lng_ref, lnb_ref, w2_ref, b2_ref, out_ref,
                 ext_ref, y_ref, *, tm, halo_is_state):
    halo = halo_ref[...]
    if not halo_is_state:
        halo = jnp.where(pl.program_id(1) > 0, halo, 0.0)
    ext_ref[0:CONV_HALO, :] = halo
    ext_ref[CONV_HALO:, :] = u_ref[...]
    n_ext = CONV_HALO + tm
    first = CONV_HALO - (CONV_WIDTH - 1)

    def lane_block(c, carry):
        cols = pl.ds(pl.multiple_of(c * LANES, LANES), LANES)
        x = ext_ref[:, cols]
        acc = jnp.zeros((tm, LANES), F32)
        for phase in range(SUBLANES):
            xp = x if phase == 0 else pltpu.roll(x, n_ext - phase, axis=0)
            for base in range(0, CONV_HALO + SUBLANES, SUBLANES):
                w = base + phase - first
                if 0 <= w < CONV_WIDTH:
                    acc = acc + xp[base:base + tm] * wdw_ref[w:w + 1, cols]
        y_ref[:, cols] = acc + bdw_ref[:, cols]
        return carry

    lax.fori_loop(0, D_MODEL // LANES, lane_block, 0)

    y = y_ref[...]
    mu = jnp.mean(y, axis=-1, keepdims=True)
    yc = y - mu
    var = jnp.mean(yc * yc, axis=-1, keepdims=True)
    yn = yc * lax.rsqrt(var + EPS_LN) * lng_ref[...] + lnb_ref[...]
    act = (yn * jax.nn.sigmoid(yn)).astype(BF16)
    out_ref[...] = x_ref[...] + _dot(act, w2_ref[...]) + b2_ref[...]


def _conv(u, state, x, p, *, layer, tm):
    batch, seq, _ = u.shape
    j_layer = layer // 2
    halo_is_state = state is not None
    if halo_is_state:
        halo_src = state
        halo_spec = pl.BlockSpec((None, CONV_HALO, D_MODEL), lambda b, i: (b, 0, 0))
    else:
        halo_src = u
        per_tile = tm // CONV_HALO
        halo_spec = pl.BlockSpec((None, CONV_HALO, D_MODEL),
                                 lambda b, i: (b, jnp.maximum(i * per_tile - 1, 0), 0))
    tile = pl.BlockSpec((None, tm, D_MODEL), lambda b, i: (b, i, 0))
    vec = _layer_vec(j_layer)
    return pl.pallas_call(
        functools.partial(_conv_kernel, tm=tm, halo_is_state=halo_is_state),
        grid=(batch, seq // tm),
        in_specs=[
            tile, halo_spec, tile,
            pl.BlockSpec((None, CONV_HALO, D_MODEL), lambda b, i: (j_layer, 0, 0)),
            vec, vec, vec,
            pl.BlockSpec((None, D_MODEL, D_MODEL), lambda b, i: (j_layer, 0, 0), pipeline_mode=pl.Buffered(1)),
            vec,
        ],
        out_specs=tile,
        out_shape=jax.ShapeDtypeStruct((batch, seq, D_MODEL), F32),
        scratch_shapes=[pltpu.VMEM((CONV_HALO + tm, D_MODEL), F32), pltpu.VMEM((tm, D_MODEL), F32)],
        compiler_params=_params("parallel", "arbitrary"),
        name="conv_pw2",
    )(u, halo_src, x, p["w_dw"], p["b_dw"], p["ln_g"], p["ln_b"], p["w_pw2"], p["b_pw2"])


def _trunk(x, cache_k, cache_v, state_conv, p, *, tm, tq, conv_tm):
    batch, seq, _ = x.shape
    m = batch * seq
    x2 = x.reshape(m, D_MODEL)
    k_all = jnp.zeros((N_SB, m, N_HEADS, HEAD_DIM), F32)
    v_all = jnp.zeros((N_SB, m, N_HEADS, HEAD_DIM), F32)
    new_conv = []
    for layer in range(DEPTH):
        j = layer // 2
        if layer % 2 == 0:
            q, k_all, v_all, kb, vb = _qkv(x2, p["norm_mix_g"], p["w_qkv"], k_all, v_all, layer=j, tm=tm)
            if cache_k is None:
                o = _attn_prompt(q, kb, vb, p["tri"], batch=batch, seq=seq, tq=tq)
            else:
                o = _attn_sample(q, kb, vb, cache_k, cache_v, p["tri"], layer=j, batch=batch, t_new=seq)
        else:
            o = None
            u = _pw1(x2, p, layer=layer, tm=tm, tn=1024).reshape(batch, seq, D_MODEL)
            if state_conv is None:
                state = None
                new_conv.append(u[:, seq - (CONV_WIDTH - 1):])
            else:
                hist = jnp.concatenate([state_conv[j], u], axis=1)
                new_conv.append(hist[:, -(CONV_WIDTH - 1):])
                state = jnp.pad(state_conv[j], ((0, 0), (CONV_HALO - (CONV_WIDTH - 1), 0), (0, 0)))
            x2 = _conv(u, state, x2.reshape(batch, seq, D_MODEL), p, layer=layer, tm=conv_tm).reshape(m, D_MODEL)
        x2 = _ffn(x2, o, p, layer=layer, tm=tm, tf=512)
    shape = (N_SB, batch, seq, N_HEADS, HEAD_DIM)
    return x2.reshape(batch, seq, D_MODEL), k_all.reshape(shape), v_all.reshape(shape), jnp.stack(new_conv)


def kernel(x_prompt, x_sample, cache_k, cache_v, state_conv, norm_mix_g, norm_ffn_g, w_qkv, w_o, w_pw1, b_pw1,
           w_dw, b_dw, ln_g, ln_b, w_pw2, b_pw2, w_gate_up, w_down, final_norm_g):
    idx = lax.broadcasted_iota(jnp.int32, (KEY_BLOCK, KEY_BLOCK), 0)

    def rows(a):
        return a.reshape(a.shape[0], 1, a.shape[1])

    p = dict(
        norm_mix_g=rows(norm_mix_g), norm_ffn_g=rows(norm_ffn_g), final_norm_g=final_norm_g.reshape(1, 1, D_MODEL),
        w_qkv=w_qkv.astype(BF16), w_o=w_o.astype(BF16), w_pw1=w_pw1.astype(BF16), b_pw1=rows(b_pw1),
        w_dw=jnp.pad(w_dw, ((0, 0), (0, CONV_HALO - CONV_WIDTH), (0, 0))), b_dw=rows(b_dw),
        ln_g=rows(ln_g), ln_b=rows(ln_b), w_pw2=w_pw2.astype(BF16), b_pw2=rows(b_pw2),
        w_gate_up=w_gate_up.astype(BF16), w_down=w_down.astype(BF16),
        tri=(idx > idx.T).astype(BF16),
    )
    y_p, k_p, v_p, conv_p = _trunk(x_prompt, None, None, None, p, tm=512, tq=1024, conv_tm=256)
    y_s, k_s, v_s, conv_s = _trunk(x_sample, cache_k, cache_v, state_conv, p, tm=256, tq=None, conv_tm=16)
    return (y_p, y_s, k_p, v_p, conv_p, k_s, v_s, conv_s)
```

```python
import functools
import math

import jax
import jax.numpy as jnp
from jax import lax
from jax.experimental import pallas as pl
from jax.experimental.pallas import tpu as pltpu

D_MODEL = 2048
N_HEADS = 16
HEAD_DIM = D_MODEL // N_HEADS
CONV_WIDTH = 31
CONV_HALO = 32
CONV_COLS_PER_STEP = 2
CONV_ROWS = 64
SUBLANES = 8
LANES = 128
D_FF = 5632
DEPTH = 4
N_SB = 2
EPS_RMS = 1e-6
EPS_LN = 1e-5
LOG2E = 1.4426950408889634
Q_SCALE = -LOG2E / math.sqrt(HEAD_DIM)
KEY_BLOCK = 256
SB_EXIT_LOG2 = -150.0
SB_DONE = -1e30
SB_BLOCKS_PER_STEP = 2
HEADS_PER_QKV_STEP = 8

F32 = jnp.float32
BF16 = jnp.bfloat16
VMEM_LIMIT = 56 * 1024 * 1024


def _params(*sem):
    return pltpu.CompilerParams(dimension_semantics=sem, vmem_limit_bytes=VMEM_LIMIT)


def _dot(a, b):
    return jnp.dot(a, b, preferred_element_type=F32)


def _dot_nt(a, b):
    return lax.dot_general(a, b, (((1,), (1,)), ((), ())), preferred_element_type=F32)


def _rms_norm(x, g):
    return x * lax.rsqrt(jnp.mean(x * x, axis=-1, keepdims=True) + EPS_RMS) * g


def _layer_vec(layer, width=D_MODEL):
    return pl.BlockSpec((None, 1, width), lambda *_: (layer, 0, 0))


def _qkv_kernel(*refs):
    x_ref, g_ref, wq_ref, wk_ref, wv_ref = refs[:5]
    q_ref, kf_ref, vf_ref, kb_ref, vb_ref, h_ref = refs[-6:]

    @pl.when(pl.program_id(1) == 0)
    def _():
        h_ref[...] = _rms_norm(x_ref[...], g_ref[...]).astype(BF16)

    h = h_ref[...]
    q_ref[...] = (_dot(h, wq_ref[...]) * Q_SCALE).astype(BF16)
    k = _dot(h, wk_ref[...])
    kf_ref[...] = pltpu.einshape("m(hd)->mhd", k, h=HEADS_PER_QKV_STEP)
    kb_ref[...] = k.astype(BF16)
    v = _dot(h, wv_ref[...])
    vf_ref[...] = pltpu.einshape("m(hd)->mhd", v, h=HEADS_PER_QKV_STEP)
    vb_ref[...] = v.astype(BF16)


def _qkv(x, g, w_qkv, k_all, v_all, *, layer, tm):
    m = x.shape[0]
    tn = HEADS_PER_QKV_STEP * HEAD_DIM
    nj = D_MODEL // tn
    out = pl.BlockSpec((tm, tn), lambda i, j: (i, j))
    slab = pl.BlockSpec((None, tm, HEADS_PER_QKV_STEP, HEAD_DIM), lambda i, j: (layer, i, j, 0))
    stacked = jax.ShapeDtypeStruct((N_SB, m, N_HEADS, HEAD_DIM), F32)
    in_specs = [
        pl.BlockSpec((tm, D_MODEL), lambda i, j: (i, 0)),
        _layer_vec(2 * layer),
        pl.BlockSpec((None, D_MODEL, tn), lambda i, j: (layer, 0, j)),
        pl.BlockSpec((None, D_MODEL, tn), lambda i, j: (layer, 0, nj + j)),
        pl.BlockSpec((None, D_MODEL, tn), lambda i, j: (layer, 0, 2 * nj + j)),
    ]
    args = [x, g, w_qkv, w_qkv, w_qkv]
    aliases = {}
    if k_all is not None:
        in_specs += [pl.BlockSpec(memory_space=pl.ANY)] * 2
        args += [k_all, v_all]
        aliases = {5: 1, 6: 2}
    return pl.pallas_call(
        _qkv_kernel,
        grid=(m // tm, nj),
        in_specs=in_specs,
        out_specs=[out, slab, slab, out, out],
        out_shape=[
            jax.ShapeDtypeStruct((m, D_MODEL), BF16),
            stacked,
            stacked,
            jax.ShapeDtypeStruct((m, D_MODEL), BF16),
            jax.ShapeDtypeStruct((m, D_MODEL), BF16),
        ],
        input_output_aliases=aliases,
        scratch_shapes=[pltpu.VMEM((tm, D_MODEL), BF16)],
        compiler_params=_params("parallel", "arbitrary"),
        name="qkv",
    )(*args)


def _sb_keep(nz2, mask):
    keep = jnp.minimum(nz2, 0.0) - jnp.log2(1.0 + jnp.exp2(-jnp.abs(nz2)))
    if mask is not None:
        keep = jnp.where(mask, keep, 0.0)
    hi = keep.astype(BF16)
    lo = (keep - hi.astype(F32)).astype(BF16)
    return keep, jnp.concatenate([hi, lo], axis=1)


def _sb_later(hilo, u):
    return _dot(hilo, jnp.concatenate([u, u], axis=0))


def _sb_finish(nz2, keep, later, carry, mask):
    bk = nz2.shape[1]
    total = later[:, 0:1] + keep[:, 0:1]
    later = later + jnp.concatenate([carry] * (bk // LANES), axis=1)
    w = jnp.exp2((keep - nz2) + later)
    if mask is not None:
        w = jnp.where(mask, w, 0.0)
    return w.astype(BF16), jnp.broadcast_to(total, carry.shape)


def _attn_prompt_kernel(q_ref, k_ref, v_ref, u_ref, o_ref, acc_ref, carry_ref, *, tq):
    bk = KEY_BLOCK
    n_sub = tq // bk
    g = pl.program_id(2)
    acc_ref[...] = jnp.zeros_like(acc_ref)
    carry_ref[...] = jnp.zeros_like(carry_ref)
    u = u_ref[...]
    diag_mask = lax.broadcasted_iota(jnp.int32, (bk, bk), 1) < lax.broadcasted_iota(jnp.int32, (bk, bk), 0)

    def step(s, first):
        chains = [(r, j) for r in range(n_sub) for j in range(SB_BLOCKS_PER_STEP)]
        rows = {c: slice(c[0] * bk, (c[0] + 1) * bk) for c in chains}
        block = {c: g * n_sub + c[0] - (s * SB_BLOCKS_PER_STEP + c[1]) for c in chains}
        keys = {c: pl.ds(pl.multiple_of(jnp.maximum(block[c], 0) * bk, bk), bk) for c in chains}
        mask = {c: diag_mask if (first and c[1] == 0) else None for c in chains}
        nz2 = {c: _dot_nt(q_ref[rows[c], :], k_ref[keys[c], :]) for c in chains}
        keep = {c: _sb_keep(nz2[c], mask[c]) for c in chains}
        later = {c: _sb_later(keep[c][1], u) for c in chains}
        for r in range(n_sub):
            carry = carry_ref[rows[(r, 0)], :]
            ws = []
            for j in range(SB_BLOCKS_PER_STEP):
                c = (r, j)
                w, total = _sb_finish(nz2[c], keep[c][0], later[c], carry, mask[c])
                ws.append(w)
                carry = jnp.where(block[c] <= 0, SB_DONE, carry + total)
            carry_ref[rows[(r, 0)], :] = carry
            vs = [v_ref[keys[(r, j)], :] for j in range(SB_BLOCKS_PER_STEP)]
            acc_ref[rows[(r, 0)], :] += _dot(jnp.concatenate(ws, axis=1), jnp.concatenate(vs, axis=0))
        return jnp.max(carry_ref[...])

    def cond(state):
        return state[1] > SB_EXIT_LOG2

    def body(state):
        s = state[0]
        return s + 1, step(s, False)

    lax.while_loop(cond, body, (jnp.int32(1), step(0, True)))
    o_ref[...] = acc_ref[...].astype(BF16)


def _attn_prompt(q, k, v, u, *, batch, seq, tq):
    nq = seq // tq
    return pl.pallas_call(
        functools.partial(_attn_prompt_kernel, tq=tq),
        grid=(batch, N_HEADS, nq),
        in_specs=[
            pl.BlockSpec((tq, HEAD_DIM), lambda b, h, i: (b * nq + i, h)),
            pl.BlockSpec((seq, HEAD_DIM), lambda b, h, i: (b, h)),
            pl.BlockSpec((seq, HEAD_DIM), lambda b, h, i: (b, h)),
            pl.BlockSpec((KEY_BLOCK, KEY_BLOCK), lambda b, h, i: (0, 0)),
        ],
        out_specs=pl.BlockSpec((tq, HEAD_DIM), lambda b, h, i: (b * nq + i, h)),
        out_shape=jax.ShapeDtypeStruct((batch * seq, D_MODEL), BF16),
        scratch_shapes=[pltpu.VMEM((tq, HEAD_DIM), F32), pltpu.VMEM((tq, LANES), F32)],
        compiler_params=_params("parallel", "parallel", "arbitrary"),
        name="attn_prompt",
    )(q, k, v, u)


def _attn_sample_kernel(q_ref, kn_ref, vn_ref, ck_ref, cv_ref, u_ref, o_ref, *, t_new):
    bk = KEY_BLOCK
    rows = N_HEADS * t_new
    n_blocks = ck_ref.shape[0] // bk
    u = u_ref[...]

    shape = (rows, D_MODEL)
    row_head = lax.broadcasted_iota(jnp.int32, shape, 0) // t_new
    col_head = lax.broadcasted_iota(jnp.int32, shape, 1) // HEAD_DIM
    qt = jnp.concatenate([q_ref[...].astype(F32)] * N_HEADS, axis=0)
    qblk = jnp.where(row_head == col_head, qt, 0.0).astype(BF16)
    pad = jnp.zeros((LANES - t_new, D_MODEL), BF16)
    kn = jnp.concatenate([kn_ref[...], pad], axis=0)
    vn = jnp.concatenate([vn_ref[...], pad], axis=0)
    kc = pltpu.einshape("shd->s(hd)", ck_ref[...]).astype(BF16)
    vc = pltpu.einshape("shd->s(hd)", cv_ref[...]).astype(BF16)

    shape = (rows, LANES)
    mask_new = lax.broadcasted_iota(jnp.int32, shape, 1) < lax.broadcasted_iota(jnp.int32, shape, 0) % t_new
    nz2_new = _dot_nt(qblk, kn)
    nz2_all = _dot_nt(qblk, kc)
    nz2 = [nz2_all[:, c * bk:(c + 1) * bk] for c in range(n_blocks)]
    keep_new = _sb_keep(nz2_new, mask_new)
    keep = [_sb_keep(z, None) for z in nz2]
    later_new = _sb_later(keep_new[1], u[:LANES, :LANES])
    later = [_sb_later(k[1], u) for k in keep]

    w_new, carry = _sb_finish(nz2_new, keep_new[0], later_new, jnp.zeros((rows, LANES), F32), mask_new)
    ws = [None] * n_blocks
    for c in reversed(range(n_blocks)):
        ws[c], total = _sb_finish(nz2[c], keep[c][0], later[c], carry, None)
        carry = carry + total
    acc = _dot(jnp.concatenate(ws + [w_new], axis=1), jnp.concatenate([vc, vn], axis=0))

    col_head = lax.broadcasted_iota(jnp.int32, (t_new, D_MODEL), 1) // HEAD_DIM
    out = jnp.zeros((t_new, D_MODEL), F32)
    for h in range(N_HEADS):
        out = jnp.where(col_head == h, acc[h * t_new:(h + 1) * t_new, :], out)
    o_ref[...] = out.astype(BF16)


def _attn_sample(q, kn, vn, cache_k, cache_v, u, *, layer, batch, t_new):
    past = cache_k.shape[2]
    new = pl.BlockSpec((t_new, D_MODEL), lambda b: (b, 0))
    cache = pl.BlockSpec((None, None, past, N_HEADS, HEAD_DIM), lambda b: (layer, b, 0, 0, 0))
    return pl.pallas_call(
        functools.partial(_attn_sample_kernel, t_new=t_new),
        grid=(batch,),
        in_specs=[new, new, new, cache, cache, pl.BlockSpec((KEY_BLOCK, KEY_BLOCK), lambda b: (0, 0))],
        out_specs=new,
        out_shape=jax.ShapeDtypeStruct((batch * t_new, D_MODEL), BF16),
        compiler_params=_params("parallel"),
        name="attn_sample",
    )(q, kn, vn, cache_k, cache_v, u)


def _dwconv_block(xcol, w_ref, cols, rows):
    n_ext = CONV_HALO + rows
    first = CONV_HALO - (CONV_WIDTH - 1)
    acc = jnp.zeros((rows, LANES), F32)
    for phase in range(SUBLANES):
        xp = xcol if phase == 0 else pltpu.roll(xcol, n_ext - phase, axis=0)
        for base in range(0, CONV_HALO + SUBLANES, SUBLANES):
            w = base + phase - first
            if 0 <= w < CONV_WIDTH:
                acc = acc + xp[base:base + rows] * w_ref[w:w + 1, cols]
    return acc


def _layer_norm_silu(y, g, b):
    mu = jnp.mean(y, axis=-1, keepdims=True)
    yc = y - mu
    var = jnp.mean(yc * yc, axis=-1, keepdims=True)
    yn = yc * lax.rsqrt(var + EPS_LN) * g + b
    return yn * jax.nn.sigmoid(yn)


def _ffn_kernel(*refs, mode, final_norm, tm, seq, n_tiles):
    if mode == "oproj":
        x_ref, o_ref, wo_ref, g_ref, wg_ref, wu_ref, wd_ref, fg_ref, out_ref, h_ref = refs
    elif mode == "conv":
        (x_ref, u_ref, halo_ref, wdw_ref, bdw_ref, lng_ref, lnb_ref, w2_ref, b2_ref,
         g_ref, wg_ref, wu_ref, wd_ref, fg_ref, out_ref, h_ref, y_ref) = refs
    else:
        x_ref, g_ref, wg_ref, wu_ref, wd_ref, fg_ref, out_ref, h_ref = refs
    i = pl.program_id(0)
    f = pl.program_id(1)

    if mode == "conv":
        @pl.when((i == 0) & (f == 0))
        def _():
            y_ref[...] = jnp.zeros_like(y_ref)

    @pl.when(f == 0)
    def _():
        x = x_ref[...]
        if mode == "oproj":
            x = x + _dot(o_ref[...], wo_ref[...])
        elif mode == "conv":
            act = _layer_norm_silu(y_ref[...], lng_ref[...], lnb_ref[...]).astype(BF16)
            x = x + _dot(act, w2_ref[...]) + b2_ref[...]
        h_ref[...] = _rms_norm(x, g_ref[...]).astype(BF16)
        out_ref[...] = x

    if mode == "conv":
        tile = jnp.minimum(i, n_tiles - 1)
        fresh = (tile * tm) % seq == 0
        for j in range(CONV_COLS_PER_STEP):
            c = jnp.minimum(f * CONV_COLS_PER_STEP + j, D_MODEL // LANES - 1)
            cols = pl.ds(pl.multiple_of(c * LANES, LANES), LANES)
            halo = jnp.where(fresh, 0.0, halo_ref[:, cols])
            xcol = jnp.concatenate([halo, u_ref[:, cols]], axis=0)
            for r0 in range(0, tm, CONV_ROWS):
                acc = _dwconv_block(xcol[r0:r0 + CONV_ROWS + CONV_HALO], wdw_ref, cols, CONV_ROWS)
                y_ref[r0:r0 + CONV_ROWS, cols] = acc + bdw_ref[:, cols]

    h = h_ref[...]
    gate = _dot(h, wg_ref[...])
    up = _dot(h, wu_ref[...])
    act = (gate * jax.nn.sigmoid(gate) * up).astype(BF16)
    out_ref[...] += _dot(act, wd_ref[...])

    if final_norm:
        @pl.when(f == pl.num_programs(1) - 1)
        def _():
            out_ref[...] = _rms_norm(out_ref[...], fg_ref[...])


def _ffn(x, o, u, p, *, layer, tm, tf, seq=None):
    m = x.shape[0]
    nf = D_FF // tf
    n_tiles = m // tm
    mode = "oproj" if o is not None else "conv" if u is not None else "plain"
    j_layer = layer // 2
    if mode == "conv":
        assert nf * CONV_COLS_PER_STEP >= D_MODEL // LANES and seq % tm == 0
        tile_of = lambda i, f: (jnp.maximum(i - 1, 0), 0)
        grid = (n_tiles + 1, nf)
    else:
        tile_of = lambda i, f: (i, 0)
        grid = (n_tiles, nf)
    row = pl.BlockSpec((tm, D_MODEL), tile_of)
    specs, args = [row], [x]
    if mode == "oproj":
        specs += [row, pl.BlockSpec((None, D_MODEL, D_MODEL), lambda i, f: (j_layer, 0, 0),
                                    pipeline_mode=pl.Buffered(1))]
        args += [o, p["w_o"]]
    elif mode == "conv":
        per_tile = tm // CONV_HALO
        vec = _layer_vec(j_layer)
        specs += [
            pl.BlockSpec((tm, D_MODEL), lambda i, f: (jnp.minimum(i, n_tiles - 1), 0),
                         pipeline_mode=pl.Buffered(1)),
            pl.BlockSpec((CONV_HALO, D_MODEL),
                         lambda i, f: (jnp.maximum(jnp.minimum(i, n_tiles - 1) * per_tile - 1, 0), 0)),
            pl.BlockSpec((None, CONV_HALO, D_MODEL), lambda i, f: (j_layer, 0, 0)),
            vec, vec, vec,
            pl.BlockSpec((None, D_MODEL, D_MODEL), lambda i, f: (j_layer, 0, 0), pipeline_mode=pl.Buffered(1)),
            vec,
        ]
        args += [u, u, p["w_dw"], p["b_dw"], p["ln_g"], p["ln_b"], p["w_pw2"], p["b_pw2"]]
    specs += [
        _layer_vec(layer),
        pl.BlockSpec((None, D_MODEL, tf), lambda i, f: (layer, 0, f)),
        pl.BlockSpec((None, D_MODEL, tf), lambda i, f: (layer, 0, nf + f)),
        pl.BlockSpec((None, tf, D_MODEL), lambda i, f: (layer, f, 0)),
        _layer_vec(0),
    ]
    args += [p["norm_ffn_g"], p["w_gate_up"], p["w_gate_up"], p["w_down"], p["final_norm_g"]]
    scratch = [pltpu.VMEM((tm, D_MODEL), BF16)]
    if mode == "conv":
        scratch.append(pltpu.VMEM((tm, D_MODEL), F32))
    return pl.pallas_call(
        functools.partial(_ffn_kernel, mode=mode, final_norm=(layer == DEPTH - 1), tm=tm, seq=seq,
                          n_tiles=n_tiles),
        grid=grid,
        in_specs=specs,
        out_specs=row,
        out_shape=jax.ShapeDtypeStruct((m, D_MODEL), F32),
        scratch_shapes=scratch,
        compiler_params=_params("arbitrary" if mode == "conv" else "parallel", "arbitrary"),
        name="ffn_" + mode,
    )(*args)


def _pw1_kernel(x_ref, g_ref, wa_ref, wg_ref, ba_ref, bg_ref, u_ref, h_ref):
    @pl.when(pl.program_id(1) == 0)
    def _():
        h_ref[...] = _rms_norm(x_ref[...], g_ref[...]).astype(BF16)

    h = h_ref[...]
    a = _dot(h, wa_ref[...]) + ba_ref[...]
    gate = _dot(h, wg_ref[...]) + bg_ref[...]
    u_ref[...] = a * jax.nn.sigmoid(gate)


def _pw1(x, p, *, layer, tm, tn):
    m = x.shape[0]
    nj = D_MODEL // tn
    j_layer = layer // 2
    return pl.pallas_call(
        _pw1_kernel,
        grid=(m // tm, nj),
        in_specs=[
            pl.BlockSpec((tm, D_MODEL), lambda i, j: (i, 0)),
            _layer_vec(layer),
            pl.BlockSpec((None, D_MODEL, tn), lambda i, j: (j_layer, 0, j)),
            pl.BlockSpec((None, D_MODEL, tn), lambda i, j: (j_layer, 0, nj + j)),
            pl.BlockSpec((None, 1, tn), lambda i, j: (j_layer, 0, j)),
            pl.BlockSpec((None, 1, tn), lambda i, j: (j_layer, 0, nj + j)),
        ],
        out_specs=pl.BlockSpec((tm, tn), lambda i, j: (i, j)),
        out_shape=jax.ShapeDtypeStruct((m, D_MODEL), F32),
        scratch_shapes=[pltpu.VMEM((tm, D_MODEL), BF16)],
        compiler_params=_params("parallel", "arbitrary"),
        name="pw1_glu",
    )(x, p["norm_mix_g"], p["w_pw1"], p["w_pw1"], p["b_pw1"], p["b_pw1"])


def _conv_kernel(u_ref, state_ref, x_ref, wdw_ref, bdw_ref, lng_ref, lnb_ref, w2_ref, b2_ref, out_ref,
                 ext_ref, y_ref, *, rows):
    ext_ref[0:CONV_HALO, :] = state_ref[...]
    ext_ref[CONV_HALO:, :] = u_ref[...]

    def lane_block(c, carry):
        cols = pl.ds(pl.multiple_of(c * LANES, LANES), LANES)
        y_ref[:, cols] = _dwconv_block(ext_ref[:, cols], wdw_ref, cols, rows) + bdw_ref[:, cols]
        return carry

    lax.fori_loop(0, D_MODEL // LANES, lane_block, 0)

    act = _layer_norm_silu(y_ref[...], lng_ref[...], lnb_ref[...]).astype(BF16)
    out_ref[...] = x_ref[...] + _dot(act, w2_ref[...]) + b2_ref[...]


def _conv(u, state, x, p, *, layer):
    batch, rows, _ = u.shape
    j_layer = layer // 2
    tile = pl.BlockSpec((None, rows, D_MODEL), lambda b: (b, 0, 0))
    vec = _layer_vec(j_layer)
    return pl.pallas_call(
        functools.partial(_conv_kernel, rows=rows),
        grid=(batch,),
        in_specs=[
            tile,
            pl.BlockSpec((None, CONV_HALO, D_MODEL), lambda b: (b, 0, 0)),
            tile,
            pl.BlockSpec((None, CONV_HALO, D_MODEL), lambda b: (j_layer, 0, 0)),
            vec, vec, vec,
            pl.BlockSpec((None, D_MODEL, D_MODEL), lambda b: (j_layer, 0, 0), pipeline_mode=pl.Buffered(1)),
            vec,
        ],
        out_specs=tile,
        out_shape=jax.ShapeDtypeStruct((batch, rows, D_MODEL), F32),
        scratch_shapes=[pltpu.VMEM((CONV_HALO + rows, D_MODEL), F32), pltpu.VMEM((rows, D_MODEL), F32)],
        compiler_params=_params("parallel"),
        name="conv_pw2",
    )(u, state, x, p["w_dw"], p["b_dw"], p["ln_g"], p["ln_b"], p["w_pw2"], p["b_pw2"])


def _trunk(x, cache_k, cache_v, state_conv, p, *, tm, tq):
    batch, seq, _ = x.shape
    m = batch * seq
    x2 = x.reshape(m, D_MODEL)
    k_all = v_all = None
    new_conv = []
    for layer in range(DEPTH):
        j = layer // 2
        o = u2 = None
        if layer % 2 == 0:
            q, k_all, v_all, kb, vb = _qkv(x2, p["norm_mix_g"], p["w_qkv"], k_all, v_all, layer=j, tm=tm)
            if cache_k is None:
                o = _attn_prompt(q, kb, vb, p["tri"], batch=batch, seq=seq, tq=tq)
            else:
                o = _attn_sample(q, kb, vb, cache_k, cache_v, p["tri"], layer=j, batch=batch, t_new=seq)
        else:
            u2 = _pw1(x2, p, layer=layer, tm=tm, tn=1024)
            u = u2.reshape(batch, seq, D_MODEL)
            if state_conv is None:
                new_conv.append(u[:, seq - (CONV_WIDTH - 1):])
            else:
                hist = jnp.concatenate([state_conv[j], u], axis=1)
                new_conv.append(hist[:, -(CONV_WIDTH - 1):])
                state = jnp.pad(state_conv[j], ((0, 0), (CONV_HALO - (CONV_WIDTH - 1), 0), (0, 0)))
                x2 = _conv(u, state, x2.reshape(batch, seq, D_MODEL), p, layer=layer).reshape(m, D_MODEL)
                u2 = None
        x2 = _ffn(x2, o, u2, p, layer=layer, tm=tm, tf=512, seq=seq)
    shape = (N_SB, batch, seq, N_HEADS, HEAD_DIM)
    return x2.reshape(batch, seq, D_MODEL), k_all.reshape(shape), v_all.reshape(shape), jnp.stack(new_conv)


def kernel(x_prompt, x_sample, cache_k, cache_v, state_conv, norm_mix_g, norm_ffn_g, w_qkv, w_o, w_pw1, b_pw1,
           w_dw, b_dw, ln_g, ln_b, w_pw2, b_pw2, w_gate_up, w_down, final_norm_g):
    idx = lax.broadcasted_iota(jnp.int32, (KEY_BLOCK, KEY_BLOCK), 0)

    def rows(a):
        return a.reshape(a.shape[0], 1, a.shape[1])

    p = dict(
        norm_mix_g=rows(norm_mix_g), norm_ffn_g=rows(norm_ffn_g), final_norm_g=final_norm_g.reshape(1, 1, D_MODEL),
        w_qkv=w_qkv.astype(BF16), w_o=w_o.astype(BF16), w_pw1=w_pw1.astype(BF16), b_pw1=rows(b_pw1),
        w_dw=jnp.pad(w_dw, ((0, 0), (0, CONV_HALO - CONV_WIDTH), (0, 0))), b_dw=rows(b_dw),
        ln_g=rows(ln_g), ln_b=rows(ln_b), w_pw2=w_pw2.astype(BF16), b_pw2=rows(b_pw2),
        w_gate_up=w_gate_up.astype(BF16), w_down=w_down.astype(BF16),
        tri=(idx > idx.T).astype(BF16),
    )
    y_p, k_p, v_p, conv_p = _trunk(x_prompt, None, None, None, p, tm=512, tq=2048)
    y_s, k_s, v_s, conv_s = _trunk(x_sample, cache_k, cache_v, state_conv, p, tm=256, tq=None)
    return (y_p, y_s, k_p, v_p, conv_p, k_s, v_s, conv_s)
```

```python
import functools
import math

import jax
import jax.numpy as jnp
from jax import lax
from jax.experimental import pallas as pl
from jax.experimental.pallas import tpu as pltpu

D_MODEL = 2048
N_HEADS = 16
HEAD_DIM = D_MODEL // N_HEADS
CONV_WIDTH = 31
CONV_HALO = 32
SUBLANES = 8
LANES = 128
D_FF = 5632
DEPTH = 4
N_SB = 2
EPS_RMS = 1e-6
EPS_LN = 1e-5
LOG2E = 1.4426950408889634
Q_SCALE = -LOG2E / math.sqrt(HEAD_DIM)
KEY_BLOCK = 256
SB_EXIT_LOG2 = -150.0
SB_DONE = -1e30
SB_BLOCKS_PER_STEP = 2
HEADS_PER_QKV_STEP = 8

F32 = jnp.float32
BF16 = jnp.bfloat16
VMEM_LIMIT = 56 * 1024 * 1024


def _params(*sem):
    return pltpu.CompilerParams(dimension_semantics=sem, vmem_limit_bytes=VMEM_LIMIT)


def _dot(a, b):
    return jnp.dot(a, b, preferred_element_type=F32)


def _dot_nt(a, b):
    return lax.dot_general(a, b, (((1,), (1,)), ((), ())), preferred_element_type=F32)


def _rms_norm(x, g):
    return x * lax.rsqrt(jnp.mean(x * x, axis=-1, keepdims=True) + EPS_RMS) * g


def _layer_vec(layer, width=D_MODEL):
    return pl.BlockSpec((None, 1, width), lambda *_: (layer, 0, 0))


def _qkv_kernel(x_ref, g_ref, wq_ref, wk_ref, wv_ref, kin_ref, vin_ref, q_ref, kf_ref, vf_ref, kb_ref, vb_ref):
    del kin_ref, vin_ref
    h = _rms_norm(x_ref[...], g_ref[...]).astype(BF16)
    q_ref[...] = (_dot(h, wq_ref[...]) * Q_SCALE).astype(BF16)
    k = _dot(h, wk_ref[...])
    kf_ref[...] = pltpu.einshape("m(hd)->mhd", k, h=HEADS_PER_QKV_STEP)
    kb_ref[...] = k.astype(BF16)
    v = _dot(h, wv_ref[...])
    vf_ref[...] = pltpu.einshape("m(hd)->mhd", v, h=HEADS_PER_QKV_STEP)
    vb_ref[...] = v.astype(BF16)


def _qkv(x, g, w_qkv, k_all, v_all, *, layer, tm):
    m = x.shape[0]
    tn = HEADS_PER_QKV_STEP * HEAD_DIM
    nj = D_MODEL // tn
    out = pl.BlockSpec((tm, tn), lambda j, i: (i, j))
    slab = pl.BlockSpec((None, tm, HEADS_PER_QKV_STEP, HEAD_DIM), lambda j, i: (layer, i, j, 0))
    untouched = pl.BlockSpec(memory_space=pl.ANY)
    return pl.pallas_call(
        _qkv_kernel,
        grid=(nj, m // tm),
        in_specs=[
            pl.BlockSpec((tm, D_MODEL), lambda j, i: (i, 0)),
            _layer_vec(2 * layer),
            pl.BlockSpec((None, D_MODEL, tn), lambda j, i: (layer, 0, j)),
            pl.BlockSpec((None, D_MODEL, tn), lambda j, i: (layer, 0, nj + j)),
            pl.BlockSpec((None, D_MODEL, tn), lambda j, i: (layer, 0, 2 * nj + j)),
            untouched, untouched,
        ],
        out_specs=[out, slab, slab, out, out],
        out_shape=[
            jax.ShapeDtypeStruct((m, D_MODEL), BF16),
            jax.ShapeDtypeStruct(k_all.shape, F32),
            jax.ShapeDtypeStruct(v_all.shape, F32),
            jax.ShapeDtypeStruct((m, D_MODEL), BF16),
            jax.ShapeDtypeStruct((m, D_MODEL), BF16),
        ],
        input_output_aliases={5: 1, 6: 2},
        compiler_params=_params("parallel", "parallel"),
        name="qkv",
    )(x, g, w_qkv, w_qkv, w_qkv, k_all, v_all)


def _sb_keep(nz2, mask):
    keep = jnp.minimum(nz2, 0.0) - jnp.log2(1.0 + jnp.exp2(-jnp.abs(nz2)))
    if mask is not None:
        keep = jnp.where(mask, keep, 0.0)
    hi = keep.astype(BF16)
    lo = (keep - hi.astype(F32)).astype(BF16)
    return keep, jnp.concatenate([hi, lo], axis=1)


def _sb_later(hilo, u):
    return _dot(hilo, jnp.concatenate([u, u], axis=0))


def _sb_finish(nz2, keep, later, carry, mask):
    bk = nz2.shape[1]
    total = later[:, 0:1] + keep[:, 0:1]
    later = later + jnp.concatenate([carry] * (bk // LANES), axis=1)
    w = jnp.exp2((keep - nz2) + later)
    if mask is not None:
        w = jnp.where(mask, w, 0.0)
    return w.astype(BF16), jnp.broadcast_to(total, carry.shape)


def _attn_prompt_kernel(q_ref, k_ref, v_ref, u_ref, o_ref, acc_ref, carry_ref, *, tq):
    bk = KEY_BLOCK
    n_sub = tq // bk
    g = pl.program_id(2)
    acc_ref[...] = jnp.zeros_like(acc_ref)
    carry_ref[...] = jnp.zeros_like(carry_ref)
    u = u_ref[...]
    diag_mask = lax.broadcasted_iota(jnp.int32, (bk, bk), 1) < lax.broadcasted_iota(jnp.int32, (bk, bk), 0)

    def step(s, first):
        chains = [(r, j) for r in range(n_sub) for j in range(SB_BLOCKS_PER_STEP)]
        rows = {c: slice(c[0] * bk, (c[0] + 1) * bk) for c in chains}
        block = {c: g * n_sub + c[0] - (s * SB_BLOCKS_PER_STEP + c[1]) for c in chains}
        keys = {c: pl.ds(pl.multiple_of(jnp.maximum(block[c], 0) * bk, bk), bk) for c in chains}
        mask = {c: diag_mask if (first and c[1] == 0) else None for c in chains}
        nz2 = {c: _dot_nt(q_ref[rows[c], :], k_ref[keys[c], :]) for c in chains}
        keep = {c: _sb_keep(nz2[c], mask[c]) for c in chains}
        later = {c: _sb_later(keep[c][1], u) for c in chains}
        for r in range(n_sub):
            carry = carry_ref[rows[(r, 0)], :]
            ws = []
            for j in range(SB_BLOCKS_PER_STEP):
                c = (r, j)
                w, total = _sb_finish(nz2[c], keep[c][0], later[c], carry, mask[c])
                ws.append(w)
                carry = jnp.where(block[c] <= 0, SB_DONE, carry + total)
            carry_ref[rows[(r, 0)], :] = carry
            vs = [v_ref[keys[(r, j)], :] for j in range(SB_BLOCKS_PER_STEP)]
            acc_ref[rows[(r, 0)], :] += _dot(jnp.concatenate(ws, axis=1), jnp.concatenate(vs, axis=0))
        return jnp.max(carry_ref[...])

    def cond(state):
        return state[1] > SB_EXIT_LOG2

    def body(state):
        s = state[0]
        return s + 1, step(s, False)

    lax.while_loop(cond, body, (jnp.int32(1), step(0, True)))
    o_ref[...] = acc_ref[...].astype(BF16)


def _attn_prompt(q, k, v, u, *, batch, seq, tq):
    nq = seq // tq
    return pl.pallas_call(
        functools.partial(_attn_prompt_kernel, tq=tq),
        grid=(batch, N_HEADS, nq),
        in_specs=[
            pl.BlockSpec((tq, HEAD_DIM), lambda b, h, i: (b * nq + i, h)),
            pl.BlockSpec((seq, HEAD_DIM), lambda b, h, i: (b, h)),
            pl.BlockSpec((seq, HEAD_DIM), lambda b, h, i: (b, h)),
            pl.BlockSpec((KEY_BLOCK, KEY_BLOCK), lambda b, h, i: (0, 0)),
        ],
        out_specs=pl.BlockSpec((tq, HEAD_DIM), lambda b, h, i: (b * nq + i, h)),
        out_shape=jax.ShapeDtypeStruct((batch * seq, D_MODEL), BF16),
        scratch_shapes=[pltpu.VMEM((tq, HEAD_DIM), F32), pltpu.VMEM((tq, LANES), F32)],
        compiler_params=_params("parallel", "parallel", "arbitrary"),
        name="attn_prompt",
    )(q, k, v, u)


def _attn_sample_kernel(q_ref, kn_ref, vn_ref, ck_ref, cv_ref, u_ref, o_ref, *, t_new):
    bk = KEY_BLOCK
    rows = N_HEADS * t_new
    n_blocks = ck_ref.shape[0] // bk
    u = u_ref[...]

    shape = (rows, D_MODEL)
    row_head = lax.broadcasted_iota(jnp.int32, shape, 0) // t_new
    col_head = lax.broadcasted_iota(jnp.int32, shape, 1) // HEAD_DIM
    qt = jnp.concatenate([q_ref[...].astype(F32)] * N_HEADS, axis=0)
    qblk = jnp.where(row_head == col_head, qt, 0.0).astype(BF16)
    pad = jnp.zeros((LANES - t_new, D_MODEL), BF16)
    kn = jnp.concatenate([kn_ref[...], pad], axis=0)
    vn = jnp.concatenate([vn_ref[...], pad], axis=0)
    kc = pltpu.einshape("shd->s(hd)", ck_ref[...]).astype(BF16)
    vc = pltpu.einshape("shd->s(hd)", cv_ref[...]).astype(BF16)

    shape = (rows, LANES)
    mask_new = lax.broadcasted_iota(jnp.int32, shape, 1) < lax.broadcasted_iota(jnp.int32, shape, 0) % t_new
    nz2_new = _dot_nt(qblk, kn)
    nz2_all = _dot_nt(qblk, kc)
    nz2 = [nz2_all[:, c * bk:(c + 1) * bk] for c in range(n_blocks)]
    keep_new = _sb_keep(nz2_new, mask_new)
    keep = [_sb_keep(z, None) for z in nz2]
    later_new = _sb_later(keep_new[1], u[:LANES, :LANES])
    later = [_sb_later(k[1], u) for k in keep]

    w_new, carry = _sb_finish(nz2_new, keep_new[0], later_new, jnp.zeros((rows, LANES), F32), mask_new)
    ws = [None] * n_blocks
    for c in reversed(range(n_blocks)):
        ws[c], total = _sb_finish(nz2[c], keep[c][0], later[c], carry, None)
        carry = carry + total
    acc = _dot(jnp.concatenate(ws + [w_new], axis=1), jnp.concatenate([vc, vn], axis=0))

    col_head = lax.broadcasted_iota(jnp.int32, (t_new, D_MODEL), 1) // HEAD_DIM
    out = jnp.zeros((t_new, D_MODEL), F32)
    for h in range(N_HEADS):
        out = jnp.where(col_head == h, acc[h * t_new:(h + 1) * t_new, :], out)
    o_ref[...] = out.astype(BF16)


def _attn_sample(q, kn, vn, cache_k, cache_v, u, *, layer, batch, t_new):
    past = cache_k.shape[2]
    new = pl.BlockSpec((t_new, D_MODEL), lambda b: (b, 0))
    cache = pl.BlockSpec((None, None, past, N_HEADS, HEAD_DIM), lambda b: (layer, b, 0, 0, 0))
    return pl.pallas_call(
        functools.partial(_attn_sample_kernel, t_new=t_new),
        grid=(batch,),
        in_specs=[new, new, new, cache, cache, pl.BlockSpec((KEY_BLOCK, KEY_BLOCK), lambda b: (0, 0))],
        out_specs=new,
        out_shape=jax.ShapeDtypeStruct((batch * t_new, D_MODEL), BF16),
        compiler_params=_params("parallel"),
        name="attn_sample",
    )(q, kn, vn, cache_k, cache_v, u)


def _ffn_kernel(*refs, final_norm, with_oproj, nf):
    if with_oproj:
        x_ref, o_ref, wo_ref, g_ref, wg_ref, wu_ref, wd_ref, fg_ref, out_ref, h_ref, act_ref = refs
    else:
        x_ref, g_ref, wg_ref, wu_ref, wd_ref, fg_ref, out_ref, h_ref, act_ref = refs
    f = pl.program_id(1)

    def gate_up(slot):
        h = h_ref[...]
        gate = _dot(h, wg_ref[...])
        up = _dot(h, wu_ref[...])
        act_ref[slot] = (gate * jax.nn.sigmoid(gate) * up).astype(BF16)

    def down(slot):
        out_ref[...] += _dot(act_ref[slot], wd_ref[...])

    @pl.when(f == 0)
    def _():
        x = x_ref[...]
        if with_oproj:
            x = x + _dot(o_ref[...], wo_ref[...])
        h_ref[...] = _rms_norm(x, g_ref[...]).astype(BF16)
        out_ref[...] = x
        gate_up(0)

    @pl.when((f > 0) & (f < nf))
    def _():
        slot = f % 2
        gate_up(slot)
        down(1 - slot)

    @pl.when(f == nf)
    def _():
        down((nf - 1) % 2)
        if final_norm:
            out_ref[...] = _rms_norm(out_ref[...], fg_ref[...])


def _ffn(x, o, p, *, layer, tm, tf):
    m = x.shape[0]
    nf = D_FF // tf
    with_oproj = o is not None
    row = pl.BlockSpec((tm, D_MODEL), lambda i, f: (i, 0))
    specs, args = [row], [x]
    if with_oproj:
        specs += [row, pl.BlockSpec((None, D_MODEL, D_MODEL), lambda i, f: (layer // 2, 0, 0),
                                    pipeline_mode=pl.Buffered(1))]
        args += [o, p["w_o"]]
    specs += [
        _layer_vec(layer),
        pl.BlockSpec((None, D_MODEL, tf), lambda i, f: (layer, 0, jnp.minimum(f, nf - 1))),
        pl.BlockSpec((None, D_MODEL, tf), lambda i, f: (layer, 0, nf + jnp.minimum(f, nf - 1))),
        pl.BlockSpec((None, tf, D_MODEL), lambda i, f: (layer, jnp.maximum(f - 1, 0), 0)),
        _layer_vec(0),
    ]
    args += [p["norm_ffn_g"], p["w_gate_up"], p["w_gate_up"], p["w_down"], p["final_norm_g"]]
    return pl.pallas_call(
        functools.partial(_ffn_kernel, final_norm=(layer == DEPTH - 1), with_oproj=with_oproj, nf=nf),
        grid=(m // tm, nf + 1),
        in_specs=specs,
        out_specs=row,
        out_shape=jax.ShapeDtypeStruct((m, D_MODEL), F32),
        scratch_shapes=[pltpu.VMEM((tm, D_MODEL), BF16), pltpu.VMEM((2, tm, tf), BF16)],
        compiler_params=_params("parallel", "arbitrary"),
        name="ffn",
    )(*args)


def _pw1_kernel(x_ref, g_ref, wa_ref, wg_ref, ba_ref, bg_ref, u_ref):
    h = _rms_norm(x_ref[...], g_ref[...]).astype(BF16)
    a = _dot(h, wa_ref[...]) + ba_ref[...]
    gate = _dot(h, wg_ref[...]) + bg_ref[...]
    u_ref[...] = a * jax.nn.sigmoid(gate)


def _pw1(x, p, *, layer, tm, tn):
    m = x.shape[0]
    nj = D_MODEL // tn
    j_layer = layer // 2
    return pl.pallas_call(
        _pw1_kernel,
        grid=(nj, m // tm),
        in_specs=[
            pl.BlockSpec((tm, D_MODEL), lambda j, i: (i, 0)),
            _layer_vec(layer),
            pl.BlockSpec((None, D_MODEL, tn), lambda j, i: (j_layer, 0, j)),
            pl.BlockSpec((None, D_MODEL, tn), lambda j, i: (j_layer, 0, nj + j)),
            pl.BlockSpec((None, 1, tn), lambda j, i: (j_layer, 0, j)),
            pl.BlockSpec((None, 1, tn), lambda j, i: (j_layer, 0, nj + j)),
        ],
        out_specs=pl.BlockSpec((tm, tn), lambda j, i: (i, j)),
        out_shape=jax.ShapeDtypeStruct((m, D_MODEL), F32),
        compiler_params=_params("parallel", "parallel"),
        name="pw1_glu",
    )(x, p["norm_mix_g"], p["w_pw1"], p["w_pw1"], p["b_pw1"], p["b_pw1"])


def _dwconv_block(xcol, w_ref, cols, rows):
    n_ext = CONV_HALO + rows
    first = CONV_HALO - (CONV_WIDTH - 1)
    acc = jnp.zeros((rows, LANES), F32)
    for phase in range(SUBLANES):
        xp = xcol if phase == 0 else pltpu.roll(xcol, n_ext - phase, axis=0)
        for base in range(0, CONV_HALO + SUBLANES, SUBLANES):
            w = base + phase - first
            if 0 <= w < CONV_WIDTH:
                acc = acc + xp[base:base + rows] * w_ref[w:w + 1, cols]
    return acc


def _conv_kernel(u_ref, halo_ref, x_ref, wdw_ref, bdw_ref, lng_ref, lnb_ref, w2_ref, b2_ref, out_ref,
                 ext_ref, y_ref, *, tm, halo_is_state):
    halo = halo_ref[...]
    if not halo_is_state:
        halo = jnp.where(pl.program_id(1) > 0, halo, 0.0)
    ext_ref[0:CONV_HALO, :] = halo
    ext_ref[CONV_HALO:, :] = u_ref[...]

    def lane_block(c, carry):
        cols = pl.ds(pl.multiple_of(c * LANES, LANES), LANES)
        y_ref[:, cols] = _dwconv_block(ext_ref[:, cols], wdw_ref, cols, tm) + bdw_ref[:, cols]
        return carry

    lax.fori_loop(0, D_MODEL // LANES, lane_block, 0)

    y = y_ref[...]
    mu = jnp.mean(y, axis=-1, keepdims=True)
    yc = y - mu
    var = jnp.mean(yc * yc, axis=-1, keepdims=True)
    yn = yc * lax.rsqrt(var + EPS_LN) * lng_ref[...] + lnb_ref[...]
    act = (yn * jax.nn.sigmoid(yn)).astype(BF16)
    out_ref[...] = x_ref[...] + _dot(act, w2_ref[...]) + b2_ref[...]


def _conv(u, state, x, p, *, layer, tm):
    batch, seq, _ = u.shape
    j_layer = layer // 2
    halo_is_state = state is not None
    if halo_is_state:
        halo_src = state
        halo_spec = pl.BlockSpec((None, CONV_HALO, D_MODEL), lambda b, i: (b, 0, 0))
    else:
        halo_src = u
        per_tile = tm // CONV_HALO
        halo_spec = pl.BlockSpec((None, CONV_HALO, D_MODEL),
                                 lambda b, i: (b, jnp.maximum(i * per_tile - 1, 0), 0))
    tile = pl.BlockSpec((None, tm, D_MODEL), lambda b, i: (b, i, 0))
    vec = _layer_vec(j_layer)
    return pl.pallas_call(
        functools.partial(_conv_kernel, tm=tm, halo_is_state=halo_is_state),
        grid=(batch, seq // tm),
        in_specs=[
            tile, halo_spec, tile,
            pl.BlockSpec((None, CONV_HALO, D_MODEL), lambda b, i: (j_layer, 0, 0)),
            vec, vec, vec,
            pl.BlockSpec((None, D_MODEL, D_MODEL), lambda b, i: (j_layer, 0, 0), pipeline_mode=pl.Buffered(1)),
            vec,
        ],
        out_specs=tile,
        out_shape=jax.ShapeDtypeStruct((batch, seq, D_MODEL), F32),
        scratch_shapes=[pltpu.VMEM((CONV_HALO + tm, D_MODEL), F32), pltpu.VMEM((tm, D_MODEL), F32)],
        compiler_params=_params("parallel", "arbitrary"),
        name="conv_pw2",
    )(u, halo_src, x, p["w_dw"], p["b_dw"], p["ln_g"], p["ln_b"], p["w_pw2"], p["b_pw2"])


def _trunk(x, cache_k, cache_v, state_conv, p, *, tm, tq, conv_tm):
    batch, seq, _ = x.shape
    m = batch * seq
    x2 = x.reshape(m, D_MODEL)
    k_all = jnp.zeros((N_SB, m, N_HEADS, HEAD_DIM), F32)
    v_all = jnp.zeros((N_SB, m, N_HEADS, HEAD_DIM), F32)
    new_conv = []
    for layer in range(DEPTH):
        j = layer // 2
        if layer % 2 == 0:
            q, k_all, v_all, kb, vb = _qkv(x2, p["norm_mix_g"], p["w_qkv"], k_all, v_all, layer=j, tm=tm)
            if cache_k is None:
                o = _attn_prompt(q, kb, vb, p["tri"], batch=batch, seq=seq, tq=tq)
            else:
                o = _attn_sample(q, kb, vb, cache_k, cache_v, p["tri"], layer=j, batch=batch, t_new=seq)
        else:
            o = None
            u = _pw1(x2, p, layer=layer, tm=tm, tn=1024).reshape(batch, seq, D_MODEL)
            if state_conv is None:
                state = None
                new_conv.append(u[:, seq - (CONV_WIDTH - 1):])
            else:
                hist = jnp.concatenate([state_conv[j], u], axis=1)
                new_conv.append(hist[:, -(CONV_WIDTH - 1):])
                state = jnp.pad(state_conv[j], ((0, 0), (CONV_HALO - (CONV_WIDTH - 1), 0), (0, 0)))
            x2 = _conv(u, state, x2.reshape(batch, seq, D_MODEL), p, layer=layer, tm=conv_tm).reshape(m, D_MODEL)
        x2 = _ffn(x2, o, p, layer=layer, tm=tm, tf=512)
    shape = (N_SB, batch, seq, N_HEADS, HEAD_DIM)
    return x2.reshape(batch, seq, D_MODEL), k_all.reshape(shape), v_all.reshape(shape), jnp.stack(new_conv)


def kernel(x_prompt, x_sample, cache_k, cache_v, state_conv, norm_mix_g, norm_ffn_g, w_qkv, w_o, w_pw1, b_pw1,
           w_dw, b_dw, ln_g, ln_b, w_pw2, b_pw2, w_gate_up, w_down, final_norm_g):
    idx = lax.broadcasted_iota(jnp.int32, (KEY_BLOCK, KEY_BLOCK), 0)

    def rows(a):
        return a.reshape(a.shape[0], 1, a.shape[1])

    p = dict(
        norm_mix_g=rows(norm_mix_g), norm_ffn_g=rows(norm_ffn_g), final_norm_g=final_norm_g.reshape(1, 1, D_MODEL),
        w_qkv=w_qkv.astype(BF16), w_o=w_o.astype(BF16), w_pw1=w_pw1.astype(BF16), b_pw1=rows(b_pw1),
        w_dw=jnp.pad(w_dw, ((0, 0), (0, CONV_HALO - CONV_WIDTH), (0, 0))), b_dw=rows(b_dw),
        ln_g=rows(ln_g), ln_b=rows(ln_b), w_pw2=w_pw2.astype(BF16), b_pw2=rows(b_pw2),
        w_gate_up=w_gate_up.astype(BF16), w_down=w_down.astype(BF16),
        tri=(idx > idx.T).astype(BF16),
    )
    y_p, k_p, v_p, conv_p = _trunk(x_prompt, None, None, None, p, tm=512, tq=2048, conv_tm=256)
    y_s, k_s, v_s, conv_s = _trunk(x_sample, cache_k, cache_v, state_conv, p, tm=256, tq=None, conv_tm=16)
    return (y_p, y_s, k_p, v_p, conv_p, k_s, v_s, conv_s)
```

```python
import functools
import math

import jax
import jax.numpy as jnp
from jax import lax
from jax.experimental import pallas as pl
from jax.experimental.pallas import tpu as pltpu

D_MODEL = 2048
N_HEADS = 16
HEAD_DIM = D_MODEL // N_HEADS
CONV_WIDTH = 31
CONV_HALO = 32
SUBLANES = 8
LANES = 128
D_FF = 5632
DEPTH = 4
N_SB = 2
EPS_RMS = 1e-6
EPS_LN = 1e-5
LOG2E = 1.4426950408889634
Q_SCALE = -LOG2E / math.sqrt(HEAD_DIM)
KEY_BLOCK = 256
SB_EXIT_LOG2 = -150.0
SB_DONE = -1e30
SB_BLOCKS_PER_STEP = 2
HEADS_PER_QKV_STEP = 8

F32 = jnp.float32
BF16 = jnp.bfloat16
VMEM_LIMIT = 56 * 1024 * 1024


def _params(*sem):
    return pltpu.CompilerParams(dimension_semantics=sem, vmem_limit_bytes=VMEM_LIMIT)


def _dot(a, b):
    return jnp.dot(a, b, preferred_element_type=F32)


def _dot_nt(a, b):
    return lax.dot_general(a, b, (((1,), (1,)), ((), ())), preferred_element_type=F32)


def _rms_norm(x, g):
    return x * lax.rsqrt(jnp.mean(x * x, axis=-1, keepdims=True) + EPS_RMS) * g


def _layer_vec(layer, width=D_MODEL):
    return pl.BlockSpec((None, 1, width), lambda *_: (layer, 0, 0))


def _qkv_kernel(x_ref, g_ref, wq_ref, wk_ref, wv_ref, kin_ref, vin_ref, q_ref, kf_ref, vf_ref, kb_ref, vb_ref):
    del kin_ref, vin_ref
    h = _rms_norm(x_ref[...], g_ref[...]).astype(BF16)
    q_ref[...] = (_dot(h, wq_ref[...]) * Q_SCALE).astype(BF16)
    k = _dot(h, wk_ref[...])
    kf_ref[...] = pltpu.einshape("m(hd)->mhd", k, h=HEADS_PER_QKV_STEP)
    kb_ref[...] = k.astype(BF16)
    v = _dot(h, wv_ref[...])
    vf_ref[...] = pltpu.einshape("m(hd)->mhd", v, h=HEADS_PER_QKV_STEP)
    vb_ref[...] = v.astype(BF16)


def _qkv(x, g, w_qkv, k_all, v_all, *, layer, tm):
    m = x.shape[0]
    tn = HEADS_PER_QKV_STEP * HEAD_DIM
    nj = D_MODEL // tn
    out = pl.BlockSpec((tm, tn), lambda j, i: (i, j))
    slab = pl.BlockSpec((None, tm, HEADS_PER_QKV_STEP, HEAD_DIM), lambda j, i: (layer, i, j, 0))
    untouched = pl.BlockSpec(memory_space=pl.ANY)
    return pl.pallas_call(
        _qkv_kernel,
        grid=(nj, m // tm),
        in_specs=[
            pl.BlockSpec((tm, D_MODEL), lambda j, i: (i, 0)),
            _layer_vec(2 * layer),
            pl.BlockSpec((None, D_MODEL, tn), lambda j, i: (layer, 0, j)),
            pl.BlockSpec((None, D_MODEL, tn), lambda j, i: (layer, 0, nj + j)),
            pl.BlockSpec((None, D_MODEL, tn), lambda j, i: (layer, 0, 2 * nj + j)),
            untouched, untouched,
        ],
        out_specs=[out, slab, slab, out, out],
        out_shape=[
            jax.ShapeDtypeStruct((m, D_MODEL), BF16),
            jax.ShapeDtypeStruct(k_all.shape, F32),
            jax.ShapeDtypeStruct(v_all.shape, F32),
            jax.ShapeDtypeStruct((m, D_MODEL), BF16),
            jax.ShapeDtypeStruct((m, D_MODEL), BF16),
        ],
        input_output_aliases={5: 1, 6: 2},
        compiler_params=_params("parallel", "parallel"),
        name="qkv",
    )(x, g, w_qkv, w_qkv, w_qkv, k_all, v_all)


def _sb_keep(nz2, mask):
    keep = jnp.minimum(nz2, 0.0) - jnp.log2(1.0 + jnp.exp2(-jnp.abs(nz2)))
    if mask is not None:
        keep = jnp.where(mask, keep, 0.0)
    hi = keep.astype(BF16)
    lo = (keep - hi.astype(F32)).astype(BF16)
    return keep, jnp.concatenate([hi, lo], axis=1)


def _sb_later(hilo, u):
    return _dot(hilo, jnp.concatenate([u, u], axis=0))


def _sb_finish(nz2, keep, later, carry, mask):
    bk = nz2.shape[1]
    total = later[:, 0:1] + keep[:, 0:1]
    later = later + jnp.concatenate([carry] * (bk // LANES), axis=1)
    w = jnp.exp2((keep - nz2) + later)
    if mask is not None:
        w = jnp.where(mask, w, 0.0)
    return w.astype(BF16), jnp.broadcast_to(total, carry.shape)


def _attn_prompt_kernel(q_ref, k_ref, v_ref, u_ref, o_ref, acc_ref, carry_ref, *, tq):
    bk = KEY_BLOCK
    n_sub = tq // bk
    g = pl.program_id(2)
    acc_ref[...] = jnp.zeros_like(acc_ref)
    carry_ref[...] = jnp.zeros_like(carry_ref)
    u = u_ref[...]
    diag_mask = lax.broadcasted_iota(jnp.int32, (bk, bk), 1) < lax.broadcasted_iota(jnp.int32, (bk, bk), 0)

    def step(s, first):
        chains = [(r, j) for r in range(n_sub) for j in range(SB_BLOCKS_PER_STEP)]
        rows = {c: slice(c[0] * bk, (c[0] + 1) * bk) for c in chains}
        block = {c: g * n_sub + c[0] - (s * SB_BLOCKS_PER_STEP + c[1]) for c in chains}
        keys = {c: pl.ds(pl.multiple_of(jnp.maximum(block[c], 0) * bk, bk), bk) for c in chains}
        mask = {c: diag_mask if (first and c[1] == 0) else None for c in chains}
        nz2 = {c: _dot_nt(q_ref[rows[c], :], k_ref[keys[c], :]) for c in chains}
        keep = {c: _sb_keep(nz2[c], mask[c]) for c in chains}
        later = {c: _sb_later(keep[c][1], u) for c in chains}
        for r in range(n_sub):
            carry = carry_ref[rows[(r, 0)], :]
            ws = []
            for j in range(SB_BLOCKS_PER_STEP):
                c = (r, j)
                w, total = _sb_finish(nz2[c], keep[c][0], later[c], carry, mask[c])
                ws.append(w)
                carry = jnp.where(block[c] <= 0, SB_DONE, carry + total)
            carry_ref[rows[(r, 0)], :] = carry
            vs = [v_ref[keys[(r, j)], :] for j in range(SB_BLOCKS_PER_STEP)]
            acc_ref[rows[(r, 0)], :] += _dot(jnp.concatenate(ws, axis=1), jnp.concatenate(vs, axis=0))
        return jnp.max(carry_ref[...])

    def cond(state):
        return state[1] > SB_EXIT_LOG2

    def body(state):
        s = state[0]
        return s + 1, step(s, False)

    lax.while_loop(cond, body, (jnp.int32(1), step(0, True)))
    o_ref[...] = acc_ref[...].astype(BF16)


def _attn_prompt(q, k, v, u, *, batch, seq, tq):
    nq = seq // tq
    return pl.pallas_call(
        functools.partial(_attn_prompt_kernel, tq=tq),
        grid=(batch, N_HEADS, nq),
        in_specs=[
            pl.BlockSpec((tq, HEAD_DIM), lambda b, h, i: (b * nq + i, h)),
            pl.BlockSpec((seq, HEAD_DIM), lambda b, h, i: (b, h)),
            pl.BlockSpec((seq, HEAD_DIM), lambda b, h, i: (b, h)),
            pl.BlockSpec((KEY_BLOCK, KEY_BLOCK), lambda b, h, i: (0, 0)),
        ],
        out_specs=pl.BlockSpec((tq, HEAD_DIM), lambda b, h, i: (b * nq + i, h)),
        out_shape=jax.ShapeDtypeStruct((batch * seq, D_MODEL), BF16),
        scratch_shapes=[pltpu.VMEM((tq, HEAD_DIM), F32), pltpu.VMEM((tq, LANES), F32)],
        compiler_params=_params("parallel", "parallel", "arbitrary"),
        name="attn_prompt",
    )(q, k, v, u)


def _attn_sample_kernel(q_ref, kn_ref, vn_ref, ck_ref, cv_ref, u_ref, o_ref, *, t_new):
    bk = KEY_BLOCK
    rows = N_HEADS * t_new
    n_blocks = ck_ref.shape[0] // bk
    u = u_ref[...]

    shape = (rows, D_MODEL)
    row_head = lax.broadcasted_iota(jnp.int32, shape, 0) // t_new
    col_head = lax.broadcasted_iota(jnp.int32, shape, 1) // HEAD_DIM
    qt = jnp.concatenate([q_ref[...].astype(F32)] * N_HEADS, axis=0)
    qblk = jnp.where(row_head == col_head, qt, 0.0).astype(BF16)
    pad = jnp.zeros((LANES - t_new, D_MODEL), BF16)
    kn = jnp.concatenate([kn_ref[...], pad], axis=0)
    vn = jnp.concatenate([vn_ref[...], pad], axis=0)
    kc = pltpu.einshape("shd->s(hd)", ck_ref[...]).astype(BF16)
    vc = pltpu.einshape("shd->s(hd)", cv_ref[...]).astype(BF16)

    shape = (rows, LANES)
    mask_new = lax.broadcasted_iota(jnp.int32, shape, 1) < lax.broadcasted_iota(jnp.int32, shape, 0) % t_new
    nz2_new = _dot_nt(qblk, kn)
    nz2_all = _dot_nt(qblk, kc)
    nz2 = [nz2_all[:, c * bk:(c + 1) * bk] for c in range(n_blocks)]
    keep_new = _sb_keep(nz2_new, mask_new)
    keep = [_sb_keep(z, None) for z in nz2]
    later_new = _sb_later(keep_new[1], u[:LANES, :LANES])
    later = [_sb_later(k[1], u) for k in keep]

    w_new, carry = _sb_finish(nz2_new, keep_new[0], later_new, jnp.zeros((rows, LANES), F32), mask_new)
    ws = [None] * n_blocks
    for c in reversed(range(n_blocks)):
        ws[c], total = _sb_finish(nz2[c], keep[c][0], later[c], carry, None)
        carry = carry + total
    acc = _dot(jnp.concatenate(ws + [w_new], axis=1), jnp.concatenate([vc, vn], axis=0))

    col_head = lax.broadcasted_iota(jnp.int32, (t_new, D_MODEL), 1) // HEAD_DIM
    out = jnp.zeros((t_new, D_MODEL), F32)
    for h in range(N_HEADS):
        out = jnp.where(col_head == h, acc[h * t_new:(h + 1) * t_new, :], out)
    o_ref[...] = out.astype(BF16)


def _attn_sample(q, kn, vn, cache_k, cache_v, u, *, layer, batch, t_new):
    past = cache_k.shape[2]
    new = pl.BlockSpec((t_new, D_MODEL), lambda b: (b, 0))
    cache = pl.BlockSpec((None, None, past, N_HEADS, HEAD_DIM), lambda b: (layer, b, 0, 0, 0))
    return pl.pallas_call(
        functools.partial(_attn_sample_kernel, t_new=t_new),
        grid=(batch,),
        in_specs=[new, new, new, cache, cache, pl.BlockSpec((KEY_BLOCK, KEY_BLOCK), lambda b: (0, 0))],
        out_specs=new,
        out_shape=jax.ShapeDtypeStruct((batch * t_new, D_MODEL), BF16),
        compiler_params=_params("parallel"),
        name="attn_sample",
    )(q, kn, vn, cache_k, cache_v, u)


def _ffn_kernel(*refs, final_norm, with_oproj):
    if with_oproj:
        x_ref, o_ref, wo_ref, g_ref, wg_ref, wu_ref, wd_ref, fg_ref, out_ref, h_ref = refs
    else:
        x_ref, g_ref, wg_ref, wu_ref, wd_ref, fg_ref, out_ref, h_ref = refs
    f = pl.program_id(1)

    @pl.when(f == 0)
    def _():
        x = x_ref[...]
        if with_oproj:
            x = x + _dot(o_ref[...], wo_ref[...])
        h_ref[...] = _rms_norm(x, g_ref[...]).astype(BF16)
        out_ref[...] = x

    h = h_ref[...]
    gate = _dot(h, wg_ref[...])
    up = _dot(h, wu_ref[...])
    act = (gate * jax.nn.sigmoid(gate) * up).astype(BF16)
    out_ref[...] += _dot(act, wd_ref[...])

    if final_norm:
        @pl.when(f == pl.num_programs(1) - 1)
        def _():
            out_ref[...] = _rms_norm(out_ref[...], fg_ref[...])


def _ffn(x, o, p, *, layer, tm, tf):
    m = x.shape[0]
    nf = D_FF // tf
    with_oproj = o is not None
    row = pl.BlockSpec((tm, D_MODEL), lambda i, f: (i, 0))
    specs, args = [row], [x]
    if with_oproj:
        specs += [row, pl.BlockSpec((None, D_MODEL, D_MODEL), lambda i, f: (layer // 2, 0, 0),
                                    pipeline_mode=pl.Buffered(1))]
        args += [o, p["w_o"]]
    specs += [
        _layer_vec(layer),
        pl.BlockSpec((None, D_MODEL, tf), lambda i, f: (layer, 0, f)),
        pl.BlockSpec((None, D_MODEL, tf), lambda i, f: (layer, 0, nf + f)),
        pl.BlockSpec((None, tf, D_MODEL), lambda i, f: (layer, f, 0)),
        _layer_vec(0),
    ]
    args += [p["norm_ffn_g"], p["w_gate_up"], p["w_gate_up"], p["w_down"], p["final_norm_g"]]
    return pl.pallas_call(
        functools.partial(_ffn_kernel, final_norm=(layer == DEPTH - 1), with_oproj=with_oproj),
        grid=(m // tm, nf),
        in_specs=specs,
        out_specs=row,
        out_shape=jax.ShapeDtypeStruct((m, D_MODEL), F32),
        scratch_shapes=[pltpu.VMEM((tm, D_MODEL), BF16)],
        compiler_params=_params("parallel", "arbitrary"),
        name="ffn",
    )(*args)


def _pw1_kernel(x_ref, g_ref, wa_ref, wg_ref, ba_ref, bg_ref, u_ref):
    h = _rms_norm(x_ref[...], g_ref[...]).astype(BF16)
    a = _dot(h, wa_ref[...]) + ba_ref[...]
    gate = _dot(h, wg_ref[...]) + bg_ref[...]
    u_ref[...] = a * jax.nn.sigmoid(gate)


def _pw1(x, p, *, layer, tm, tn):
    m = x.shape[0]
    nj = D_MODEL // tn
    j_layer = layer // 2
    return pl.pallas_call(
        _pw1_kernel,
        grid=(nj, m // tm),
        in_specs=[
            pl.BlockSpec((tm, D_MODEL), lambda j, i: (i, 0)),
            _layer_vec(layer),
            pl.BlockSpec((None, D_MODEL, tn), lambda j, i: (j_layer, 0, j)),
            pl.BlockSpec((None, D_MODEL, tn), lambda j, i: (j_layer, 0, nj + j)),
            pl.BlockSpec((None, 1, tn), lambda j, i: (j_layer, 0, j)),
            pl.BlockSpec((None, 1, tn), lambda j, i: (j_layer, 0, nj + j)),
        ],
        out_specs=pl.BlockSpec((tm, tn), lambda j, i: (i, j)),
        out_shape=jax.ShapeDtypeStruct((m, D_MODEL), F32),
        compiler_params=_params("parallel", "parallel"),
        name="pw1_glu",
    )(x, p["norm_mix_g"], p["w_pw1"], p["w_pw1"], p["b_pw1"], p["b_pw1"])


def _dwconv_block(xcol, w_ref, cols, rows):
    n_ext = CONV_HALO + rows
    first = CONV_HALO - (CONV_WIDTH - 1)
    acc = jnp.zeros((rows, LANES), F32)
    for phase in range(SUBLANES):
        xp = xcol if phase == 0 else pltpu.roll(xcol, n_ext - phase, axis=0)
        for base in range(0, CONV_HALO + SUBLANES, SUBLANES):
            w = base + phase - first
            if 0 <= w < CONV_WIDTH:
                acc = acc + xp[base:base + rows] * w_ref[w:w + 1, cols]
    return acc


def _conv_kernel(u_ref, halo_ref, x_ref, wdw_ref, bdw_ref, lng_ref, lnb_ref, w2_ref, b2_ref, out_ref,
                 ext_ref, y_ref, *, tm, halo_is_state):
    halo = halo_ref[...]
    if not halo_is_state:
        halo = jnp.where(pl.program_id(1) > 0, halo, 0.0)
    ext_ref[0:CONV_HALO, :] = halo
    ext_ref[CONV_HALO:, :] = u_ref[...]

    def lane_block(c, carry):
        cols = pl.ds(pl.multiple_of(c * LANES, LANES), LANES)
        y_ref[:, cols] = _dwconv_block(ext_ref[:, cols], wdw_ref, cols, tm) + bdw_ref[:, cols]
        return carry

    lax.fori_loop(0, D_MODEL // LANES, lane_block, 0)

    y = y_ref[...]
    mu = jnp.mean(y, axis=-1, keepdims=True)
    yc = y - mu
    var = jnp.mean(yc * yc, axis=-1, keepdims=True)
    yn = yc * lax.rsqrt(var + EPS_LN) * lng_ref[...] + lnb_ref[...]
    act = (yn * jax.nn.sigmoid(yn)).astype(BF16)
    out_ref[...] = x_ref[...] + _dot(act, w2_ref[...]) + b2_ref[...]


def _conv(u, state, x, p, *, layer, tm):
    batch, seq, _ = u.shape
    j_layer = layer // 2
    halo_is_state = state is not None
    if halo_is_state:
        halo_src = state
        halo_spec = pl.BlockSpec((None, CONV_HALO, D_MODEL), lambda b, i: (b, 0, 0))
    else:
        halo_src = u
        per_tile = tm // CONV_HALO
        halo_spec = pl.BlockSpec((None, CONV_HALO, D_MODEL),
                                 lambda b, i: (b, jnp.maximum(i * per_tile - 1, 0), 0))
    tile = pl.BlockSpec((None, tm, D_MODEL), lambda b, i: (b, i, 0))
    vec = _layer_vec(j_layer)
    return pl.pallas_call(
        functools.partial(_conv_kernel, tm=tm, halo_is_state=halo_is_state),
        grid=(batch, seq // tm),
        in_specs=[
            tile, halo_spec, tile,
            pl.BlockSpec((None, CONV_HALO, D_MODEL), lambda b, i: (j_layer, 0, 0)),
            vec, vec, vec,
            pl.BlockSpec((None, D_MODEL, D_MODEL), lambda b, i: (j_layer, 0, 0), pipeline_mode=pl.Buffered(1)),
            vec,
        ],
        out_specs=tile,
        out_shape=jax.ShapeDtypeStruct((batch, seq, D_MODEL), F32),
        scratch_shapes=[pltpu.VMEM((CONV_HALO + tm, D_MODEL), F32), pltpu.VMEM((tm, D_MODEL), F32)],
        compiler_params=_params("parallel", "arbitrary"),
        name="conv_pw2",
    )(u, halo_src, x, p["w_dw"], p["b_dw"], p["ln_g"], p["ln_b"], p["w_pw2"], p["b_pw2"])


def _trunk(x, cache_k, cache_v, state_conv, p, *, tm, tq, conv_tm):
    batch, seq, _ = x.shape
    m = batch * seq
    x2 = x.reshape(m, D_MODEL)
    k_all = jnp.zeros((N_SB, m, N_HEADS, HEAD_DIM), F32)
    v_all = jnp.zeros((N_SB, m, N_HEADS, HEAD_DIM), F32)
    new_conv = []
    for layer in range(DEPTH):
        j = layer // 2
        if layer % 2 == 0:
            q, k_all, v_all, kb, vb = _qkv(x2, p["norm_mix_g"], p["w_qkv"], k_all, v_all, layer=j, tm=tm)
            if cache_k is None:
                o = _attn_prompt(q, kb, vb, p["tri"], batch=batch, seq=seq, tq=tq)
            else:
                o = _attn_sample(q, kb, vb, cache_k, cache_v, p["tri"], layer=j, batch=batch, t_new=seq)
        else:
            o = None
            u = _pw1(x2, p, layer=layer, tm=tm, tn=1024).reshape(batch, seq, D_MODEL)
            if state_conv is None:
                state = None
                new_conv.append(u[:, seq - (CONV_WIDTH - 1):])
            else:
                hist = jnp.concatenate([state_conv[j], u], axis=1)
                new_conv.append(hist[:, -(CONV_WIDTH - 1):])
                state = jnp.pad(state_conv[j], ((0, 0), (CONV_HALO - (CONV_WIDTH - 1), 0), (0, 0)))
            x2 = _conv(u, state, x2.reshape(batch, seq, D_MODEL), p, layer=layer, tm=conv_tm).reshape(m, D_MODEL)
        x2 = _ffn(x2, o, p, layer=layer, tm=tm, tf=512)
    shape = (N_SB, batch, seq, N_HEADS, HEAD_DIM)
    return x2.reshape(batch, seq, D_MODEL), k_all.reshape(shape), v_all.reshape(shape), jnp.stack(new_conv)


def kernel(x_prompt, x_sample, cache_k, cache_v, state_conv, norm_mix_g, norm_ffn_g, w_qkv, w_o, w_pw1, b_pw1,
           w_dw, b_dw, ln_g, ln_b, w_pw2, b_pw2, w_gate_up, w_down, final_norm_g):
    idx = lax.broadcasted_iota(jnp.int32, (KEY_BLOCK, KEY_BLOCK), 0)

    def rows(a):
        return a.reshape(a.shape[0], 1, a.shape[1])

    p = dict(
        norm_mix_g=rows(norm_mix_g), norm_ffn_g=rows(norm_ffn_g), final_norm_g=final_norm_g.reshape(1, 1, D_MODEL),
        w_qkv=w_qkv.astype(BF16), w_o=w_o.astype(BF16), w_pw1=w_pw1.astype(BF16), b_pw1=rows(b_pw1),
        w_dw=jnp.pad(w_dw, ((0, 0), (0, CONV_HALO - CONV_WIDTH), (0, 0))), b_dw=rows(b_dw),
        ln_g=rows(ln_g), ln_b=rows(ln_b), w_pw2=w_pw2.astype(BF16), b_pw2=rows(b_pw2),
        w_gate_up=w_gate_up.astype(BF16), w_down=w_down.astype(BF16),
        tri=(idx > idx.T).astype(BF16),
    )
    y_p, k_p, v_p, conv_p = _trunk(x_prompt, None, None, None, p, tm=512, tq=2048, conv_tm=256)
    y_s, k_s, v_s, conv_s = _trunk(x_sample, cache_k, cache_v, state_conv, p, tm=256, tq=None, conv_tm=16)
    return (y_p, y_s, k_p, v_p, conv_p, k_s, v_s, conv_s)
```

```python
import functools
import math

import jax
import jax.numpy as jnp
from jax import lax
from jax.experimental import pallas as pl
from jax.experimental.pallas import tpu as pltpu

D_MODEL = 2048
N_HEADS = 16
HEAD_DIM = D_MODEL // N_HEADS
CONV_WIDTH = 31
CONV_HALO = 32
SUBLANES = 8
LANES = 128
D_FF = 5632
DEPTH = 4
N_SB = 2
EPS_RMS = 1e-6
EPS_LN = 1e-5
LOG2E = 1.4426950408889634
Q_SCALE = -LOG2E / math.sqrt(HEAD_DIM)
KEY_BLOCK = 256
SB_EXIT_LOG2 = -150.0
SB_DONE = -1e30
SB_BLOCKS_PER_STEP = 2
HEADS_PER_QKV_STEP = 8

F32 = jnp.float32
BF16 = jnp.bfloat16
VMEM_LIMIT = 56 * 1024 * 1024


def _params(*sem):
    return pltpu.CompilerParams(dimension_semantics=sem, vmem_limit_bytes=VMEM_LIMIT)


def _dot(a, b):
    return jnp.dot(a, b, preferred_element_type=F32)


def _dot_nt(a, b):
    return lax.dot_general(a, b, (((1,), (1,)), ((), ())), preferred_element_type=F32)


def _rms_norm(x, g):
    return x * lax.rsqrt(jnp.mean(x * x, axis=-1, keepdims=True) + EPS_RMS) * g


def _layer_vec(layer, width=D_MODEL):
    return pl.BlockSpec((None, 1, width), lambda *_: (layer, 0, 0))


def _qkv_kernel(*refs, n_prev):
    x_ref, g_ref, wq_ref, wk_ref, wv_ref = refs[:5]
    prev = refs[5:5 + 2 * n_prev]
    q_ref, kf_ref, vf_ref, kb_ref, vb_ref = refs[5 + 2 * n_prev:]
    h = _rms_norm(x_ref[...], g_ref[...]).astype(BF16)
    q_ref[...] = (_dot(h, wq_ref[...]) * Q_SCALE).astype(BF16)
    k = _dot(h, wk_ref[...])
    v = _dot(h, wv_ref[...])
    k3 = pltpu.einshape("m(hd)->mhd", k, h=HEADS_PER_QKV_STEP)
    v3 = pltpu.einshape("m(hd)->mhd", v, h=HEADS_PER_QKV_STEP)
    if n_prev:
        for l in range(n_prev):
            kf_ref[l] = prev[2 * l][...]
            vf_ref[l] = prev[2 * l + 1][...]
        kf_ref[n_prev] = k3
        vf_ref[n_prev] = v3
    else:
        kf_ref[...] = k3
        vf_ref[...] = v3
    kb_ref[...] = k.astype(BF16)
    vb_ref[...] = v.astype(BF16)


def _qkv(x, g, w_qkv, prev_kv, *, layer, tm):
    m = x.shape[0]
    tn = HEADS_PER_QKV_STEP * HEAD_DIM
    nj = D_MODEL // tn
    n_prev = len(prev_kv)
    once = pl.Buffered(1)
    out = pl.BlockSpec((tm, tn), lambda j, i: (i, j))
    tile = pl.BlockSpec((tm, HEADS_PER_QKV_STEP, HEAD_DIM), lambda j, i: (i, j, 0))
    if n_prev:
        kv_spec = pl.BlockSpec((n_prev + 1, tm, HEADS_PER_QKV_STEP, HEAD_DIM), lambda j, i: (0, i, j, 0))
        kv_shape = jax.ShapeDtypeStruct((n_prev + 1, m, N_HEADS, HEAD_DIM), F32)
    else:
        kv_spec = tile
        kv_shape = jax.ShapeDtypeStruct((m, N_HEADS, HEAD_DIM), F32)
    return pl.pallas_call(
        functools.partial(_qkv_kernel, n_prev=n_prev),
        grid=(nj, m // tm),
        in_specs=[
            pl.BlockSpec((tm, D_MODEL), lambda j, i: (i, 0)),
            _layer_vec(2 * layer),
            pl.BlockSpec((None, D_MODEL, tn), lambda j, i: (layer, 0, j), pipeline_mode=once),
            pl.BlockSpec((None, D_MODEL, tn), lambda j, i: (layer, 0, nj + j), pipeline_mode=once),
            pl.BlockSpec((None, D_MODEL, tn), lambda j, i: (layer, 0, 2 * nj + j), pipeline_mode=once),
        ] + [tile] * (2 * n_prev),
        out_specs=[out, kv_spec, kv_spec, out, out],
        out_shape=[
            jax.ShapeDtypeStruct((m, D_MODEL), BF16),
            kv_shape,
            kv_shape,
            jax.ShapeDtypeStruct((m, D_MODEL), BF16),
            jax.ShapeDtypeStruct((m, D_MODEL), BF16),
        ],
        compiler_params=_params("parallel", "parallel"),
        name="qkv",
    )(x, g, w_qkv, w_qkv, w_qkv, *[a for pair in prev_kv for a in pair])


def _sb_keep(nz2, mask):
    keep = jnp.minimum(nz2, 0.0) - jnp.log2(1.0 + jnp.exp2(-jnp.abs(nz2)))
    if mask is not None:
        keep = jnp.where(mask, keep, 0.0)
    hi = keep.astype(BF16)
    lo = (keep - hi.astype(F32)).astype(BF16)
    return keep, jnp.concatenate([hi, lo], axis=1)


def _sb_later(hilo, u):
    return _dot(hilo, jnp.concatenate([u, u], axis=0))


def _sb_finish(nz2, keep, later, carry, mask):
    bk = nz2.shape[1]
    total = later[:, 0:1] + keep[:, 0:1]
    later = later + jnp.concatenate([carry] * (bk // LANES), axis=1)
    w = jnp.exp2((keep - nz2) + later)
    if mask is not None:
        w = jnp.where(mask, w, 0.0)
    return w.astype(BF16), jnp.broadcast_to(total, carry.shape)


def _attn_prompt_kernel(q_ref, k_ref, v_ref, u_ref, o_ref, acc_ref, carry_ref, *, tq):
    bk = KEY_BLOCK
    n_sub = tq // bk
    g = pl.program_id(2)
    acc_ref[...] = jnp.zeros_like(acc_ref)
    carry_ref[...] = jnp.zeros_like(carry_ref)
    u = u_ref[...]
    diag_mask = lax.broadcasted_iota(jnp.int32, (bk, bk), 1) < lax.broadcasted_iota(jnp.int32, (bk, bk), 0)

    def step(s, first):
        chains = [(r, j) for r in range(n_sub) for j in range(SB_BLOCKS_PER_STEP)]
        rows = {c: slice(c[0] * bk, (c[0] + 1) * bk) for c in chains}
        block = {c: g * n_sub + c[0] - (s * SB_BLOCKS_PER_STEP + c[1]) for c in chains}
        keys = {c: pl.ds(pl.multiple_of(jnp.maximum(block[c], 0) * bk, bk), bk) for c in chains}
        mask = {c: diag_mask if (first and c[1] == 0) else None for c in chains}
        nz2 = {c: _dot_nt(q_ref[rows[c], :], k_ref[keys[c], :]) for c in chains}
        keep = {c: _sb_keep(nz2[c], mask[c]) for c in chains}
        later = {c: _sb_later(keep[c][1], u) for c in chains}
        for r in range(n_sub):
            carry = carry_ref[rows[(r, 0)], :]
            ws = []
            for j in range(SB_BLOCKS_PER_STEP):
                c = (r, j)
                w, total = _sb_finish(nz2[c], keep[c][0], later[c], carry, mask[c])
                ws.append(w)
                carry = jnp.where(block[c] <= 0, SB_DONE, carry + total)
            carry_ref[rows[(r, 0)], :] = carry
            vs = [v_ref[keys[(r, j)], :] for j in range(SB_BLOCKS_PER_STEP)]
            acc_ref[rows[(r, 0)], :] += _dot(jnp.concatenate(ws, axis=1), jnp.concatenate(vs, axis=0))
        return jnp.max(carry_ref[...])

    def cond(state):
        return state[1] > SB_EXIT_LOG2

    def body(state):
        s = state[0]
        return s + 1, step(s, False)

    lax.while_loop(cond, body, (jnp.int32(1), step(0, True)))
    o_ref[...] = acc_ref[...].astype(BF16)


def _attn_prompt(q, k, v, u, *, batch, seq, tq):
    nq = seq // tq
    return pl.pallas_call(
        functools.partial(_attn_prompt_kernel, tq=tq),
        grid=(batch, N_HEADS, nq),
        in_specs=[
            pl.BlockSpec((tq, HEAD_DIM), lambda b, h, i: (b * nq + i, h)),
            pl.BlockSpec((seq, HEAD_DIM), lambda b, h, i: (b, h)),
            pl.BlockSpec((seq, HEAD_DIM), lambda b, h, i: (b, h)),
            pl.BlockSpec((KEY_BLOCK, KEY_BLOCK), lambda b, h, i: (0, 0)),
        ],
        out_specs=pl.BlockSpec((tq, HEAD_DIM), lambda b, h, i: (b * nq + i, h)),
        out_shape=jax.ShapeDtypeStruct((batch * seq, D_MODEL), BF16),
        scratch_shapes=[pltpu.VMEM((tq, HEAD_DIM), F32), pltpu.VMEM((tq, LANES), F32)],
        compiler_params=_params("parallel", "parallel", "arbitrary"),
        name="attn_prompt",
    )(q, k, v, u)


def _attn_sample_kernel(q_ref, kn_ref, vn_ref, ck_ref, cv_ref, u_ref, o_ref, *, t_new):
    bk = KEY_BLOCK
    rows = N_HEADS * t_new
    n_blocks = ck_ref.shape[0] // bk
    u = u_ref[...]

    shape = (rows, D_MODEL)
    row_head = lax.broadcasted_iota(jnp.int32, shape, 0) // t_new
    col_head = lax.broadcasted_iota(jnp.int32, shape, 1) // HEAD_DIM
    qt = jnp.concatenate([q_ref[...].astype(F32)] * N_HEADS, axis=0)
    qblk = jnp.where(row_head == col_head, qt, 0.0).astype(BF16)
    pad = jnp.zeros((LANES - t_new, D_MODEL), BF16)
    kn = jnp.concatenate([kn_ref[...], pad], axis=0)
    vn = jnp.concatenate([vn_ref[...], pad], axis=0)
    kc = pltpu.einshape("shd->s(hd)", ck_ref[...]).astype(BF16)
    vc = pltpu.einshape("shd->s(hd)", cv_ref[...]).astype(BF16)

    shape = (rows, LANES)
    mask_new = lax.broadcasted_iota(jnp.int32, shape, 1) < lax.broadcasted_iota(jnp.int32, shape, 0) % t_new
    nz2_new = _dot_nt(qblk, kn)
    nz2_all = _dot_nt(qblk, kc)
    nz2 = [nz2_all[:, c * bk:(c + 1) * bk] for c in range(n_blocks)]
    keep_new = _sb_keep(nz2_new, mask_new)
    keep = [_sb_keep(z, None) for z in nz2]
    later_new = _sb_later(keep_new[1], u[:LANES, :LANES])
    later = [_sb_later(k[1], u) for k in keep]

    w_new, carry = _sb_finish(nz2_new, keep_new[0], later_new, jnp.zeros((rows, LANES), F32), mask_new)
    ws = [None] * n_blocks
    for c in reversed(range(n_blocks)):
        ws[c], total = _sb_finish(nz2[c], keep[c][0], later[c], carry, None)
        carry = carry + total
    acc = _dot(jnp.concatenate(ws + [w_new], axis=1), jnp.concatenate([vc, vn], axis=0))

    col_head = lax.broadcasted_iota(jnp.int32, (t_new, D_MODEL), 1) // HEAD_DIM
    out = jnp.zeros((t_new, D_MODEL), F32)
    for h in range(N_HEADS):
        out = jnp.where(col_head == h, acc[h * t_new:(h + 1) * t_new, :], out)
    o_ref[...] = out.astype(BF16)


def _attn_sample(q, kn, vn, cache_k, cache_v, u, *, layer, batch, t_new):
    past = cache_k.shape[2]
    new = pl.BlockSpec((t_new, D_MODEL), lambda b: (b, 0))
    cache = pl.BlockSpec((None, None, past, N_HEADS, HEAD_DIM), lambda b: (layer, b, 0, 0, 0))
    return pl.pallas_call(
        functools.partial(_attn_sample_kernel, t_new=t_new),
        grid=(batch,),
        in_specs=[new, new, new, cache, cache, pl.BlockSpec((KEY_BLOCK, KEY_BLOCK), lambda b: (0, 0))],
        out_specs=new,
        out_shape=jax.ShapeDtypeStruct((batch * t_new, D_MODEL), BF16),
        compiler_params=_params("parallel"),
        name="attn_sample",
    )(q, kn, vn, cache_k, cache_v, u)


def _ffn_kernel(*refs, final_norm, with_oproj):
    if with_oproj:
        x_ref, o_ref, wo_ref, g_ref, wg_ref, wu_ref, wd_ref, fg_ref, out_ref, h_ref = refs
    else:
        x_ref, g_ref, wg_ref, wu_ref, wd_ref, fg_ref, out_ref, h_ref = refs
    f = pl.program_id(1)

    @pl.when(f == 0)
    def _():
        x = x_ref[...]
        if with_oproj:
            x = x + _dot(o_ref[...], wo_ref[...])
        h_ref[...] = _rms_norm(x, g_ref[...]).astype(BF16)
        out_ref[...] = x

    h = h_ref[...]
    half = wg_ref.shape[1] // 2
    acts = []
    for cols in (slice(0, half), slice(half, 2 * half)):
        gate = _dot(h, wg_ref[:, cols])
        up = _dot(h, wu_ref[:, cols])
        acts.append((gate * jax.nn.sigmoid(gate) * up).astype(BF16))
    out_ref[...] += _dot(jnp.concatenate(acts, axis=1), wd_ref[...])

    if final_norm:
        @pl.when(f == pl.num_programs(1) - 1)
        def _():
            out_ref[...] = _rms_norm(out_ref[...], fg_ref[...])


def _ffn(x, o, p, *, layer, tm, tf):
    m = x.shape[0]
    nf = D_FF // tf
    with_oproj = o is not None
    row = pl.BlockSpec((tm, D_MODEL), lambda i, f: (i, 0))
    specs, args = [row], [x]
    if with_oproj:
        specs += [row, pl.BlockSpec((None, D_MODEL, D_MODEL), lambda i, f: (layer // 2, 0, 0),
                                    pipeline_mode=pl.Buffered(1))]
        args += [o, p["w_o"]]
    specs += [
        _layer_vec(layer),
        pl.BlockSpec((None, D_MODEL, tf), lambda i, f: (layer, 0, f)),
        pl.BlockSpec((None, D_MODEL, tf), lambda i, f: (layer, 0, nf + f)),
        pl.BlockSpec((None, tf, D_MODEL), lambda i, f: (layer, f, 0)),
        _layer_vec(0),
    ]
    args += [p["norm_ffn_g"], p["w_gate_up"], p["w_gate_up"], p["w_down"], p["final_norm_g"]]
    return pl.pallas_call(
        functools.partial(_ffn_kernel, final_norm=(layer == DEPTH - 1), with_oproj=with_oproj),
        grid=(m // tm, nf),
        in_specs=specs,
        out_specs=row,
        out_shape=jax.ShapeDtypeStruct((m, D_MODEL), F32),
        scratch_shapes=[pltpu.VMEM((tm, D_MODEL), BF16)],
        compiler_params=_params("parallel", "arbitrary"),
        name="ffn",
    )(*args)


def _pw1_kernel(x_ref, g_ref, wa_ref, wg_ref, ba_ref, bg_ref, u_ref):
    half = x_ref.shape[0] // 2
    for rows in (slice(0, half), slice(half, 2 * half)):
        h = _rms_norm(x_ref[rows, :], g_ref[...]).astype(BF16)
        a = _dot(h, wa_ref[...]) + ba_ref[...]
        gate = _dot(h, wg_ref[...]) + bg_ref[...]
        u_ref[rows, :] = a * jax.nn.sigmoid(gate)


def _pw1(x, p, *, layer, tm, tn):
    m = x.shape[0]
    nj = D_MODEL // tn
    j_layer = layer // 2
    return pl.pallas_call(
        _pw1_kernel,
        grid=(nj, m // tm),
        in_specs=[
            pl.BlockSpec((tm, D_MODEL), lambda j, i: (i, 0)),
            _layer_vec(layer),
            pl.BlockSpec((None, D_MODEL, tn), lambda j, i: (j_layer, 0, j)),
            pl.BlockSpec((None, D_MODEL, tn), lambda j, i: (j_layer, 0, nj + j)),
            pl.BlockSpec((None, 1, tn), lambda j, i: (j_layer, 0, j)),
            pl.BlockSpec((None, 1, tn), lambda j, i: (j_layer, 0, nj + j)),
        ],
        out_specs=pl.BlockSpec((tm, tn), lambda j, i: (i, j)),
        out_shape=jax.ShapeDtypeStruct((m, D_MODEL), F32),
        compiler_params=_params("parallel", "parallel"),
        name="pw1_glu",
    )(x, p["norm_mix_g"], p["w_pw1"], p["w_pw1"], p["b_pw1"], p["b_pw1"])


def _dwconv_block(xcol, w_ref, cols, rows):
    n_ext = CONV_HALO + rows
    first = CONV_HALO - (CONV_WIDTH - 1)
    acc = jnp.zeros((rows, LANES), F32)
    for phase in range(SUBLANES):
        xp = xcol if phase == 0 else pltpu.roll(xcol, n_ext - phase, axis=0)
        for base in range(0, CONV_HALO + SUBLANES, SUBLANES):
            w = base + phase - first
            if 0 <= w < CONV_WIDTH:
                acc = acc + xp[base:base + rows] * w_ref[w:w + 1, cols]
    return acc


def _conv_kernel(u_ref, halo_ref, x_ref, wdw_ref, bdw_ref, lng_ref, lnb_ref, w2_ref, b2_ref, out_ref,
                 ext_ref, y_ref, *, tm, halo_is_state):
    halo = halo_ref[...]
    if not halo_is_state:
        halo = jnp.where(pl.program_id(1) > 0, halo, 0.0)
    ext_ref[0:CONV_HALO, :] = halo
    ext_ref[CONV_HALO:, :] = u_ref[...]

    def lane_block(c, carry):
        cols = pl.ds(pl.multiple_of(c * LANES, LANES), LANES)
        y_ref[:, cols] = _dwconv_block(ext_ref[:, cols], wdw_ref, cols, tm) + bdw_ref[:, cols]
        return carry

    lax.fori_loop(0, D_MODEL // LANES, lane_block, 0)

    y = y_ref[...]
    mu = jnp.mean(y, axis=-1, keepdims=True)
    yc = y - mu
    var = jnp.mean(yc * yc, axis=-1, keepdims=True)
    yn = yc * lax.rsqrt(var + EPS_LN) * lng_ref[...] + lnb_ref[...]
    act = (yn * jax.nn.sigmoid(yn)).astype(BF16)
    out_ref[...] = x_ref[...] + _dot(act, w2_ref[...]) + b2_ref[...]


def _conv(u, state, x, p, *, layer, tm):
    batch, seq, _ = u.shape
    j_layer = layer // 2
    halo_is_state = state is not None
    if halo_is_state:
        halo_src = state
        halo_spec = pl.BlockSpec((None, CONV_HALO, D_MODEL), lambda b, i: (b, 0, 0))
    else:
        halo_src = u
        per_tile = tm // CONV_HALO
        halo_spec = pl.BlockSpec((None, CONV_HALO, D_MODEL),
                                 lambda b, i: (b, jnp.maximum(i * per_tile - 1, 0), 0))
    tile = pl.BlockSpec((None, tm, D_MODEL), lambda b, i: (b, i, 0))
    vec = _layer_vec(j_layer)
    return pl.pallas_call(
        functools.partial(_conv_kernel, tm=tm, halo_is_state=halo_is_state),
        grid=(batch, seq // tm),
        in_specs=[
            tile, halo_spec, tile,
            pl.BlockSpec((None, CONV_HALO, D_MODEL), lambda b, i: (j_layer, 0, 0)),
            vec, vec, vec,
            pl.BlockSpec((None, D_MODEL, D_MODEL), lambda b, i: (j_layer, 0, 0), pipeline_mode=pl.Buffered(1)),
            vec,
        ],
        out_specs=tile,
        out_shape=jax.ShapeDtypeStruct((batch, seq, D_MODEL), F32),
        scratch_shapes=[pltpu.VMEM((CONV_HALO + tm, D_MODEL), F32), pltpu.VMEM((tm, D_MODEL), F32)],
        compiler_params=_params("parallel", "arbitrary"),
        name="conv_pw2",
    )(u, halo_src, x, p["w_dw"], p["b_dw"], p["ln_g"], p["ln_b"], p["w_pw2"], p["b_pw2"])


def _trunk(x, cache_k, cache_v, state_conv, p, *, tm, tq, conv_tm):
    batch, seq, _ = x.shape
    m = batch * seq
    x2 = x.reshape(m, D_MODEL)
    kv = []
    new_conv = []
    for layer in range(DEPTH):
        j = layer // 2
        if layer % 2 == 0:
            last = j == N_SB - 1
            q, kf, vf, kb, vb = _qkv(x2, p["norm_mix_g"], p["w_qkv"], kv if last else (), layer=j, tm=tm)
            kv = [(kf, vf)] if last else kv + [(kf, vf)]
            if cache_k is None:
                o = _attn_prompt(q, kb, vb, p["tri"], batch=batch, seq=seq, tq=tq)
            else:
                o = _attn_sample(q, kb, vb, cache_k, cache_v, p["tri"], layer=j, batch=batch, t_new=seq)
        else:
            o = None
            u = _pw1(x2, p, layer=layer, tm=tm, tn=1024).reshape(batch, seq, D_MODEL)
            if state_conv is None:
                state = None
                new_conv.append(u[:, seq - (CONV_WIDTH - 1):])
            else:
                hist = jnp.concatenate([state_conv[j], u], axis=1)
                new_conv.append(hist[:, -(CONV_WIDTH - 1):])
                state = jnp.pad(state_conv[j], ((0, 0), (CONV_HALO - (CONV_WIDTH - 1), 0), (0, 0)))
            x2 = _conv(u, state, x2.reshape(batch, seq, D_MODEL), p, layer=layer, tm=conv_tm).reshape(m, D_MODEL)
        x2 = _ffn(x2, o, p, layer=layer, tm=tm, tf=512)
    (k_all, v_all), = kv
    shape = (N_SB, batch, seq, N_HEADS, HEAD_DIM)
    return x2.reshape(batch, seq, D_MODEL), k_all.reshape(shape), v_all.reshape(shape), jnp.stack(new_conv)


def kernel(x_prompt, x_sample, cache_k, cache_v, state_conv, norm_mix_g, norm_ffn_g, w_qkv, w_o, w_pw1, b_pw1,
           w_dw, b_dw, ln_g, ln_b, w_pw2, b_pw2, w_gate_up, w_down, final_norm_g):
    idx = lax.broadcasted_iota(jnp.int32, (KEY_BLOCK, KEY_BLOCK), 0)

    def rows(a):
        return a.reshape(a.shape[0], 1, a.shape[1])

    p = dict(
        norm_mix_g=rows(norm_mix_g), norm_ffn_g=rows(norm_ffn_g), final_norm_g=final_norm_g.reshape(1, 1, D_MODEL),
        w_qkv=w_qkv.astype(BF16), w_o=w_o.astype(BF16), w_pw1=w_pw1.astype(BF16), b_pw1=rows(b_pw1),
        w_dw=jnp.pad(w_dw, ((0, 0), (0, CONV_HALO - CONV_WIDTH), (0, 0))), b_dw=rows(b_dw),
        ln_g=rows(ln_g), ln_b=rows(ln_b), w_pw2=w_pw2.astype(BF16), b_pw2=rows(b_pw2),
        w_gate_up=w_gate_up.astype(BF16), w_down=w_down.astype(BF16),
        tri=(idx > idx.T).astype(BF16),
    )
    y_p, k_p, v_p, conv_p = _trunk(x_prompt, None, None, None, p, tm=512, tq=2048, conv_tm=256)
    y_s, k_s, v_s, conv_s = _trunk(x_sample, cache_k, cache_v, state_conv, p, tm=256, tq=None, conv_tm=16)
    return (y_p, y_s, k_p, v_p, conv_p, k_s, v_s, conv_s)
```
